```python
import math
import jax, jax.numpy as jnp
from jax import lax
import numpy as np

D_MODEL = 2048
BATCH = 4
SEQ = 2048
DEPTH = 4
DEC_BATCH = 128
DEC_SEQ = 1
PAST_LEN = 16384
PAGE_SIZE = 128

N_MIXERS = 2
POOL_WINDOWS = (2, 4, 8, 16)
N_POOL_GROUPS = len(POOL_WINDOWS)
POOL_GROUP = D_MODEL // N_POOL_GROUPS
POOL_CTX = max(POOL_WINDOWS) - 1
GLA_HEADS = 4
GLA_KEY_DIM = D_MODEL // 2
GLA_VAL_DIM = D_MODEL
GLA_DK = GLA_KEY_DIM // GLA_HEADS
GLA_DV = GLA_VAL_DIM // GLA_HEADS
GLA_GATE_RANK = 16
GLA_GATE_TEMP = 16.0
GLA_CHUNK = 16
GLA_IN = 2 * GLA_KEY_DIM + 2 * GLA_VAL_DIM + GLA_GATE_RANK
D_FF = 4 * D_MODEL
LN_EPS = 1e-5
RMS_EPS = 1e-5
DN_ALPHA = (2 * DEPTH) ** 0.25
DN_BETA = (8 * DEPTH) ** -0.25
N_POOL_LAYERS = (DEPTH + 1) // 2
N_GLA_LAYERS = DEPTH // 2

kernel_name = "hybrid_pool_gla_deepnorm_step"


def layer_norm(x, g, b):
    xf = x.astype(jnp.float32)
    mu = jnp.mean(xf, axis=-1, keepdims=True)
    var = jnp.mean(jnp.square(xf - mu), axis=-1, keepdims=True)
    y = (xf - mu) * lax.rsqrt(var + LN_EPS) * g.astype(jnp.float32) + b.astype(jnp.float32)
    return y.astype(x.dtype)


def pool_mixer(x_ctx, x, pos0, w_groups, scale):
    n_ctx = x_ctx.shape[1]
    L = x.shape[1]
    xe = jnp.concatenate([x_ctx, x.astype(x_ctx.dtype)], axis=1)
    cs = jnp.cumsum(xe.astype(jnp.float32), axis=1)
    pos = pos0 + n_ctx + jnp.arange(L, dtype=jnp.int32)
    xf = x.astype(jnp.float32)
    outs = []
    for g, w in enumerate(POOL_WINDOWS):
        sl = slice(g * POOL_GROUP, (g + 1) * POOL_GROUP)
        csp = jnp.pad(cs[..., sl], ((0, 0), (w, 0), (0, 0)))
        win = csp[:, n_ctx + w:] - csp[:, n_ctx:n_ctx + L]
        count = jnp.minimum(pos + 1, w).astype(jnp.float32)[None, :, None]
        p = win / count - xf[..., sl]
        outs.append(jnp.einsum('blc,cd->bld', p, w_groups[g].astype(jnp.float32)))
    y = jnp.concatenate(outs, axis=-1) * scale.astype(jnp.float32)
    new_state = xe[:, -POOL_CTX:]
    return y.astype(x.dtype), new_state


def gla_mixer(x, s0, w_in, w_gate_up, gate_bias, norm_w, w_out):
    B, L, _ = x.shape
    K, V = GLA_KEY_DIM, GLA_VAL_DIM
    proj = jnp.einsum('bld,de->ble', x, w_in)
    q, k, v, og, gk_low = jnp.split(proj, [K, 2 * K, 2 * K + V, 2 * K + 2 * V], axis=-1)
    gk = jnp.einsum('blr,rk->blk', gk_low, w_gate_up) + gate_bias
    log_a = jax.nn.log_sigmoid(gk.astype(jnp.float32)) / GLA_GATE_TEMP
    q = q.astype(jnp.float32).reshape(B, L, GLA_HEADS, GLA_DK) * (GLA_DK ** -0.5)
    k = k.astype(jnp.float32).reshape(B, L, GLA_HEADS, GLA_DK)
    v = v.astype(jnp.float32).reshape(B, L, GLA_HEADS, GLA_DV)
    log_a = log_a.reshape(B, L, GLA_HEADS, GLA_DK)
    C = min(GLA_CHUNK, L)
    n_chunks = -(-L // C)
    pad = n_chunks * C - L

    def to_chunks(t):
        t = jnp.pad(t, ((0, 0), (0, pad), (0, 0), (0, 0)))
        t = t.reshape(B, n_chunks, C, GLA_HEADS, t.shape[-1])
        return jnp.transpose(t, (1, 0, 3, 2, 4))

    mask = jnp.tril(jnp.ones((C, C), dtype=bool))[:, :, None]

    def step(S, inp):
        qc, kc, vc, lac = inp
        bcum = jnp.cumsum(lac, axis=2)
        diff = bcum[:, :, :, None, :] - bcum[:, :, None, :, :]
        decay = jnp.exp(jnp.where(mask, diff, -jnp.inf))
        attn = jnp.einsum('bhik,bhjk,bhijk->bhij', qc, kc, decay)
        o = jnp.einsum('bhij,bhjv->bhiv', attn, vc) + \
            jnp.einsum('bhik,bhkv->bhiv', qc * jnp.exp(bcum), S)
        b_last = bcum[:, :, -1:, :]
        S_new = S * jnp.exp(b_last[:, :, 0, :, None]) + \
            jnp.einsum('bhjk,bhjv->bhkv', kc * jnp.exp(b_last - bcum), vc)
        return S_new, o

    S_final, o = lax.scan(step, s0.astype(jnp.float32),
                          (to_chunks(q), to_chunks(k), to_chunks(v), to_chunks(log_a)))
    o = jnp.transpose(o, (1, 0, 3, 2, 4)).reshape(B, n_chunks * C, GLA_HEADS, GLA_DV)[:, :L]
    o = o * lax.rsqrt(jnp.mean(jnp.square(o), axis=-1, keepdims=True) + RMS_EPS) * norm_w.astype(jnp.float32)
    o = o.reshape(B, L, V) * jax.nn.silu(og.astype(jnp.float32))
    y = jnp.einsum('blv,vd->bld', o.astype(x.dtype), w_out)
    return y, S_final.astype(x.dtype)


def sq_relu_mlp(x, w1, b1, w2, b2):
    h = jnp.square(jax.nn.relu(jnp.einsum('bld,df->blf', x, w1) + b1))
    return jnp.einsum('blf,fd->bld', h, w2) + b2


def trunk(x, pool_ctx, gla_init, pos0, pool_w, pool_scale, gla_w_in, gla_w_gate_up, gla_gate_bias,
          gla_norm_w, gla_w_out, ln_mix_g, ln_mix_b, mlp_w1, mlp_b1, mlp_w2, mlp_b2, ln_ffn_g, ln_ffn_b):
    pool_states, gla_states = [], []
    for i in range(DEPTH):
        j = i // N_MIXERS
        if i % N_MIXERS == 0:
            ctx = x[:, :0] if pool_ctx is None else pool_ctx[j]
            h, st = pool_mixer(ctx, x, pos0, pool_w[j], pool_scale[j])
            pool_states.append(st)
        else:
            s0 = (jnp.zeros((x.shape[0], GLA_HEADS, GLA_DK, GLA_DV), jnp.float32)
                  if gla_init is None else gla_init[j])
            h, st = gla_mixer(x, s0, gla_w_in[j], gla_w_gate_up[j], gla_gate_bias[j],
                              gla_norm_w[j], gla_w_out[j])
            gla_states.append(st)
        x = layer_norm(DN_ALPHA * x + h, ln_mix_g[i], ln_mix_b[i])
        x = layer_norm(DN_ALPHA * x + sq_relu_mlp(x, mlp_w1[i], mlp_b1[i], mlp_w2[i], mlp_b2[i]),
                       ln_ffn_g[i], ln_ffn_b[i])
    return x, jnp.stack(pool_states), jnp.stack(gla_states)


def setup_inputs(seed: int = 0) -> dict:
    key = jax.random.key(seed)
    ks = jax.random.split(key, 24)
    f32 = jnp.float32
    nrm = lambda k, shape, s: jax.random.normal(k, shape, f32) * s
    return {
        "x_prompt": nrm(ks[0], (BATCH, SEQ, D_MODEL), 1.0),
        "x_sample": nrm(ks[1], (DEC_BATCH, DEC_SEQ, D_MODEL), 1.0),
        "state_pool": nrm(ks[2], (N_POOL_LAYERS, DEC_BATCH, POOL_CTX, D_MODEL), 1.0),
        "state_gla": nrm(ks[3], (N_GLA_LAYERS, DEC_BATCH, GLA_HEADS, GLA_DK, GLA_DV), 0.5),
        "pool_w": nrm(ks[4], (N_POOL_LAYERS, N_POOL_GROUPS, POOL_GROUP, POOL_GROUP), POOL_GROUP ** -0.5 * DN_BETA),
        "pool_scale": 1.0 + nrm(ks[5], (N_POOL_LAYERS, D_MODEL), 0.1),
        "gla_w_in": nrm(ks[6], (N_GLA_LAYERS, D_MODEL, GLA_IN), D_MODEL ** -0.5),
        "gla_w_gate_up": nrm(ks[7], (N_GLA_LAYERS, GLA_GATE_RANK, GLA_KEY_DIM), GLA_GATE_RANK ** -0.5),
        "gla_gate_bias": nrm(ks[8], (N_GLA_LAYERS, GLA_KEY_DIM), 0.1),
        "gla_norm_w": 1.0 + nrm(ks[9], (N_GLA_LAYERS, GLA_DV), 0.1),
        "gla_w_out": nrm(ks[10], (N_GLA_LAYERS, GLA_VAL_DIM, D_MODEL), GLA_VAL_DIM ** -0.5 * DN_BETA),
        "ln_mix_g": 1.0 + nrm(ks[11], (DEPTH, D_MODEL), 0.1),
        "ln_mix_b": nrm(ks[12], (DEPTH, D_MODEL), 0.02),
        "mlp_w1": nrm(ks[13], (DEPTH, D_MODEL, D_FF), D_MODEL ** -0.5),
        "mlp_b1": nrm(ks[14], (DEPTH, D_FF), 0.02),
        "mlp_w2": nrm(ks[15], (DEPTH, D_FF, D_MODEL), D_FF ** -0.5 * DN_BETA),
        "mlp_b2": nrm(ks[16], (DEPTH, D_MODEL), 0.02),
        "ln_ffn_g": 1.0 + nrm(ks[17], (DEPTH, D_MODEL), 0.1),
        "ln_ffn_b": nrm(ks[18], (DEPTH, D_MODEL), 0.02),
    }


def reference(x_prompt, x_sample, state_pool, state_gla, pool_w, pool_scale, gla_w_in, gla_w_gate_up,
              gla_gate_bias, gla_norm_w, gla_w_out, ln_mix_g, ln_mix_b, mlp_w1, mlp_b1, mlp_w2, mlp_b2,
              ln_ffn_g, ln_ffn_b):
    y_prompt, new_pool_prompt, new_gla_prompt = trunk(
        x_prompt, None, None, 0, pool_w, pool_scale, gla_w_in, gla_w_gate_up, gla_gate_bias,
        gla_norm_w, gla_w_out, ln_mix_g, ln_mix_b, mlp_w1, mlp_b1, mlp_w2, mlp_b2, ln_ffn_g, ln_ffn_b)
    y_sample, new_pool_sample, new_gla_sample = trunk(
        x_sample, state_pool, state_gla, PAST_LEN - POOL_CTX, pool_w, pool_scale, gla_w_in,
        gla_w_gate_up, gla_gate_bias, gla_norm_w, gla_w_out, ln_mix_g, ln_mix_b, mlp_w1, mlp_b1,
        mlp_w2, mlp_b2, ln_ffn_g, ln_ffn_b)
    return (y_prompt, y_sample, new_pool_prompt, new_gla_prompt, new_pool_sample, new_gla_sample)
```

```python
import functools

import jax
import jax.numpy as jnp
from jax import lax
from jax.experimental import pallas as pl
from jax.experimental.pallas import tpu as pltpu

F32 = jnp.float32
BF16 = jnp.bfloat16

D_MODEL = 2048
DEPTH = 4
PAST_LEN = 16384
POOL_WINDOWS = (2, 4, 8, 16)
POOL_GROUP = D_MODEL // len(POOL_WINDOWS)
POOL_CTX = max(POOL_WINDOWS) - 1
GLA_HEADS = 4
GLA_KEY_DIM = D_MODEL // 2
GLA_VAL_DIM = D_MODEL
GLA_DK = GLA_KEY_DIM // GLA_HEADS
GLA_DV = GLA_VAL_DIM // GLA_HEADS
GLA_GATE_RANK = 16
GLA_GATE_TEMP = 16.0
GLA_CHUNK = 16
D_FF = 4 * D_MODEL
LN_EPS = 1e-5
RMS_EPS = 1e-5
DN_ALPHA = (2 * DEPTH) ** 0.25

LANES = 128
MXU_DIM = 256
VMEM_LIMIT_BYTES = 56 * 1024 * 1024

PROJ_QKVG = 2 * GLA_KEY_DIM + 2 * GLA_VAL_DIM
PROJ_TILE = 5 * MXU_DIM
PROJ_PAD = -(-(PROJ_QKVG + LANES) // PROJ_TILE) * PROJ_TILE
GLA_BLOCK = 128
CHUNKS_PER_BLOCK = GLA_BLOCK // GLA_CHUNK
POOL_ROWS = 256
MLP_ROWS = 256


def _params(*sem):
    return pltpu.CompilerParams(dimension_semantics=sem, vmem_limit_bytes=VMEM_LIMIT_BYTES)


def _layer_norm(v, g, b):
    mu = jnp.mean(v, axis=-1, keepdims=True)
    c = v - mu
    var = jnp.mean(c * c, axis=-1, keepdims=True)
    return c * lax.rsqrt(var + LN_EPS) * g + b


def _split_bf16(z, parts):
    out = []
    for _ in range(parts - 1):
        hi = z.astype(BF16)
        out.append(hi)
        z = z - hi.astype(F32)
    out.append(z.astype(BF16))
    return out


def _mask_dot(mask_bf16, z, parts):
    return sum(jnp.dot(mask_bf16, t, preferred_element_type=F32) for t in _split_bf16(z, parts))


def _log_sigmoid(x):
    return jnp.minimum(x, 0.0) - jnp.log1p(jnp.exp(-jnp.abs(x)))


def _pool_ln_kernel(x_ref, halo_ref, bm_ref, bh_ref, w_ref, scale_ref, g_ref, b_ref, o_ref, *, tile_rows):
    j = pl.program_id(1)
    rb = min(POOL_ROWS, tile_rows)
    row = lax.broadcasted_iota(jnp.int32, (rb, 1), 0)
    for g, w in enumerate(POOL_WINDOWS):
        cols = slice(g * POOL_GROUP, (g + 1) * POOL_GROUP)
        for r in range(tile_rows // rb):
            xb = x_ref[r * rb:(r + 1) * rb, cols]
            if r == 0:
                halo = jnp.where(j == 0, 0.0, halo_ref[:, cols])
            else:
                halo = x_ref[r * rb - 16:r * rb, cols]
            win = _mask_dot(bm_ref[g], xb, 2)
            head = win[:16] + _mask_dot(bh_ref[g], halo, 2)
            win = jnp.concatenate([head, win[16:]], axis=0)
            count = jnp.minimum(j * tile_rows + r * rb + row + 1, w).astype(F32)
            p = win / count - xb
            y = jnp.dot(p.astype(BF16), w_ref[g], preferred_element_type=F32)
            o_ref[r * rb:(r + 1) * rb, cols] = y * scale_ref[:, cols]
    ln_rows = min(128, tile_rows)
    for r in range(tile_rows // ln_rows):
        rows = slice(r * ln_rows, (r + 1) * ln_rows)
        v = DN_ALPHA * x_ref[rows, :] + o_ref[rows, :]
        o_ref[rows, :] = _layer_norm(v, g_ref[...], b_ref[...])


def _band_matrices(rb):
    i = jnp.arange(rb)[:, None]
    jj = jnp.arange(rb)[None, :]
    bm = jnp.stack([((i - jj >= 0) & (i - jj < w)) for w in POOL_WINDOWS]).astype(BF16)
    i16 = jnp.arange(16)[:, None]
    j16 = jnp.arange(16)[None, :]
    bh = jnp.stack([(j16 >= i16 + 17 - w) for w in POOL_WINDOWS]).astype(BF16)
    return bm, bh


def _pool_ln_prompt(x2d, batch, seq, w_bf16, scale, g, b):
    tile_rows = min(1024, seq)
    nt = seq // tile_rows
    bm, bh = _band_matrices(min(POOL_ROWS, tile_rows))
    const = lambda *shape: pl.BlockSpec(shape, lambda bi, j: (0,) * len(shape))
    return pl.pallas_call(
        functools.partial(_pool_ln_kernel, tile_rows=tile_rows),
        grid=(batch, nt),
        in_specs=[
            pl.BlockSpec((tile_rows, D_MODEL), lambda bi, j: (bi * nt + j, 0)),
            pl.BlockSpec((16, D_MODEL), lambda bi, j: (jnp.maximum((bi * nt + j) * (tile_rows // 16) - 1, 0), 0)),
            const(*bm.shape), const(*bh.shape), const(*w_bf16.shape),
            const(1, D_MODEL), const(1, D_MODEL), const(1, D_MODEL),
        ],
        out_specs=pl.BlockSpec((tile_rows, D_MODEL), lambda bi, j: (bi * nt + j, 0)),
        out_shape=jax.ShapeDtypeStruct(x2d.shape, F32),
        compiler_params=_params("parallel", "arbitrary"),
        name="pool_ln_prompt",
    )(x2d, x2d, bm, bh, w_bf16, scale, g, b)


def _pool_ln_sample_kernel(x_ref, st_ref, w_ref, scale_ref, g_ref, b_ref, o_ref):
    x = x_ref[...]
    ys = []
    for g, w in enumerate(POOL_WINDOWS):
        xg = x[:, g * POOL_GROUP:(g + 1) * POOL_GROUP]
        win = xg
        for r in range(POOL_CTX - (w - 1), POOL_CTX):
            lo = r * D_MODEL + g * POOL_GROUP
            win = win + st_ref[:, lo:lo + POOL_GROUP]
        p = win / float(min(PAST_LEN + 1, w)) - xg
        ys.append(jnp.dot(p.astype(BF16), w_ref[g], preferred_element_type=F32))
    y = jnp.concatenate(ys, axis=-1) * scale_ref[...]
    o_ref[...] = _layer_norm(DN_ALPHA * x + y, g_ref[...], b_ref[...])


def _pool_ln_sample(xs, state_flat, layer, w_bf16, scale, g, b):
    bs = xs.shape[0]
    tb = min(32, bs)
    const = lambda *shape: pl.BlockSpec(shape, lambda i: (0,) * len(shape))
    return pl.pallas_call(
        _pool_ln_sample_kernel,
        grid=(bs // tb,),
        in_specs=[
            pl.BlockSpec((tb, D_MODEL), lambda i: (i, 0)),
            pl.BlockSpec((None, tb, POOL_CTX * D_MODEL), lambda i: (layer, i, 0)),
            const(*w_bf16.shape), const(1, D_MODEL), const(1, D_MODEL), const(1, D_MODEL),
        ],
        out_specs=pl.BlockSpec((tb, D_MODEL), lambda i: (i, 0)),
        out_shape=jax.ShapeDtypeStruct(xs.shape, F32),
        compiler_params=_params("parallel"),
        name="pool_ln_sample",
    )(xs, state_flat, w_bf16, scale, g, b)


def _mlp_ln_kernel(x_ref, w1_ref, b1_ref, w2_ref, b2_ref, g_ref, b_ref, o_ref, xb_ref):
    f = pl.program_id(1)

    @pl.when(f == 0)
    def _():
        xb_ref[...] = x_ref[...].astype(BF16)

        o_ref[...] = jnp.zeros_like(o_ref)

    tm = x_ref.shape[0]
    rb = min(MLP_ROWS, tm)
    for r in range(tm // rb):
        rows = slice(r * rb, (r + 1) * rb)
        h = jnp.dot(xb_ref[rows, :], w1_ref[...], preferred_element_type=F32) + b1_ref[...]
        h = jnp.square(jnp.maximum(h, 0.0))
        o_ref[rows, :] += jnp.dot(h.astype(BF16), w2_ref[...], preferred_element_type=F32)

    @pl.when(f == pl.num_programs(1) - 1)
    def _():
        for r in range(tm // rb):
            rows = slice(r * rb, (r + 1) * rb)
            v = DN_ALPHA * x_ref[rows, :] + (o_ref[rows, :] + b2_ref[...])
            o_ref[rows, :] = _layer_norm(v, g_ref[...], b_ref[...])


def _mlp_ln(x2d, w1, b1, w2, b2, g, b):
    m = x2d.shape[0]
    tm = min(1024, m)
    tf = 512
    return pl.pallas_call(
        _mlp_ln_kernel,
        grid=(m // tm, D_FF // tf),
        in_specs=[
            pl.BlockSpec((tm, D_MODEL), lambda i, f: (i, 0)),
            pl.BlockSpec((D_MODEL, tf), lambda i, f: (0, f)),
            pl.BlockSpec((1, tf), lambda i, f: (0, f)),
            pl.BlockSpec((tf, D_MODEL), lambda i, f: (f, 0)),
            pl.BlockSpec((1, D_MODEL), lambda i, f: (0, 0)),
            pl.BlockSpec((1, D_MODEL), lambda i, f: (0, 0)),
            pl.BlockSpec((1, D_MODEL), lambda i, f: (0, 0)),
        ],
        out_specs=pl.BlockSpec((tm, D_MODEL), lambda i, f: (i, 0)),
        out_shape=jax.ShapeDtypeStruct(x2d.shape, F32),
        scratch_shapes=[pltpu.VMEM((tm, D_MODEL), BF16)],
        compiler_params=_params("parallel", "arbitrary"),
        name="mlp_ln",
    )(x2d, w1, b1, w2, b2, g, b)


def _proj_kernel(x_ref, w_ref, o_ref, xb_ref):
    @pl.when(pl.program_id(1) == 0)
    def _():
        xb_ref[...] = x_ref[...].astype(BF16)

    o_ref[...] = jnp.dot(xb_ref[...], w_ref[...], preferred_element_type=F32)


def _gla_proj(x2d, w_in_pad):
    m = x2d.shape[0]
    tm = min(1024, m)
    return pl.pallas_call(
        _proj_kernel,
        grid=(m // tm, PROJ_PAD // PROJ_TILE),
        in_specs=[
            pl.BlockSpec((tm, D_MODEL), lambda i, n: (i, 0)),
            pl.BlockSpec((D_MODEL, PROJ_TILE), lambda i, n: (0, n)),
        ],
        out_specs=pl.BlockSpec((tm, PROJ_TILE), lambda i, n: (i, n)),
        out_shape=jax.ShapeDtypeStruct((m, PROJ_PAD), F32),
        scratch_shapes=[pltpu.VMEM((tm, D_MODEL), BF16)],
        compiler_params=_params("parallel", "arbitrary"),
        name="gla_proj",
    )(x2d, w_in_pad)


def _rms_gate(o, og, nw):
    o = o * lax.rsqrt(jnp.mean(o * o, axis=-1, keepdims=True) + RMS_EPS) * nw
    return o * (og * jax.nn.sigmoid(og))


def _gla_block(r0, q_ref, k_ref, v_ref, og_ref, gl_ref, wgu_ref, gb_ref, nw_ref, tril_ref,
               o_ref, s_ref, qs_ref, bc_ref, a_ref):
    rows = pl.ds(r0, GLA_BLOCK)
    q = q_ref[rows, :] * (GLA_DK ** -0.5)
    k = k_ref[rows, :]
    v = v_ref[rows, :].astype(BF16)
    gk = jnp.dot(gl_ref[rows, :].astype(BF16), wgu_ref[...], preferred_element_type=F32) + gb_ref[...]
    la = _log_sigmoid(gk) / GLA_GATE_TEMP
    bc = _mask_dot(tril_ref[...], la, 3)
    bc_ref[...] = bc
    qs_ref[...] = q

    tot = [bc_ref[c * GLA_CHUNK + GLA_CHUNK - 1:(c + 1) * GLA_CHUNK, :] for c in range(CHUNKS_PER_BLOCK)]
    before = [jnp.zeros_like(tot[0])]
    for c in range(1, CHUNKS_PER_BLOCK):
        before.append(before[-1] + tot[c - 1])
    after = [jnp.zeros_like(tot[0])]
    for c in range(CHUNKS_PER_BLOCK - 2, -1, -1):
        after.insert(0, after[0] + tot[c + 1])
    total = before[-1] + tot[-1]

    def per_chunk(vals):
        return jnp.concatenate([jnp.broadcast_to(t, (GLA_CHUNK, GLA_DK)) for t in vals], axis=0)

    qd = q * jnp.exp(bc)
    kd = k * jnp.exp(per_chunk(tot) - bc)

    s0 = s_ref[...]
    o = jnp.dot((qd * per_chunk([jnp.exp(t) for t in before])).astype(BF16), s0.astype(BF16),
                preferred_element_type=F32)

    lhs = []
    for jc in range(CHUNKS_PER_BLOCK - 1):
        lo = (jc + 1) * GLA_CHUNK
        between = per_chunk(before[jc + 1:]) - before[jc + 1]
        lhs.append((qd[lo:] * jnp.exp(between)).astype(BF16))
    pair = lax.dot_general(jnp.concatenate(lhs, axis=0), kd.astype(BF16), (((1,), (1,)), ((), ())),
                           preferred_element_type=F32)
    a_ref[...] = jnp.zeros_like(a_ref)
    off = 0
    for jc in range(CHUNKS_PER_BLOCK - 1):
        lo = (jc + 1) * GLA_CHUNK
        n = GLA_BLOCK - lo
        lane = lax.broadcasted_iota(jnp.int32, (n, GLA_BLOCK), 1)
        keep = (lane >= jc * GLA_CHUNK) & (lane < lo)
        a_ref[lo:, :] += jnp.where(keep, pair[off:off + n], 0.0)
        off += n

    half = GLA_CHUNK // 2

    def chunk_scores(c, carry):
        cr = pl.ds(pl.multiple_of(c * GLA_CHUNK, GLA_CHUNK), GLA_CHUNK)
        qc = qs_ref[cr, :]
        kc = k_ref[pl.ds(pl.multiple_of(r0 + c * GLA_CHUNK, GLA_CHUNK), GLA_CHUNK), :]
        bcc = bc_ref[cr, :]
        full = jnp.zeros((GLA_CHUNK, GLA_BLOCK), F32)
        tail = jnp.zeros((half, GLA_BLOCK), F32)
        for j in range(GLA_CHUNK):
            s = 0 if j < half else half
            row = lax.broadcasted_iota(jnp.int32, (GLA_CHUNK - s, GLA_DK), 0) + s
            lane = lax.broadcasted_iota(jnp.int32, (GLA_CHUNK - s, GLA_BLOCK), 1)
            d = bcc[s:] - bcc[j:j + 1]
            e = jnp.exp(jnp.where(row >= j, d, -jnp.inf))
            t = qc[s:] * e * kc[j:j + 1]
            val = jnp.where(lane == c * GLA_CHUNK + j, jnp.sum(t, axis=1, keepdims=True), 0.0)
            if j < half:
                full = full + val
            else:
                tail = tail + val
        a_ref[cr, :] += jnp.concatenate([full[:half], full[half:] + tail], axis=0)
        return carry

    lax.fori_loop(0, CHUNKS_PER_BLOCK, chunk_scores, 0)

    o = o + jnp.dot(a_ref[...].astype(BF16), v, preferred_element_type=F32)
    o_ref[rows, :] = _rms_gate(o, og_ref[rows, :], nw_ref[...]).astype(o_ref.dtype)

    kend = (kd * per_chunk([jnp.exp(t) for t in after])).astype(BF16)
    upd = lax.dot_general(kend, v, (((0,), (0,)), ((), ())), preferred_element_type=F32)
    gcol = jnp.transpose(jnp.broadcast_to(jnp.exp(total), (LANES, GLA_DK)))
    for n in range(GLA_DV // LANES):
        cols = slice(n * LANES, (n + 1) * LANES)
        s_ref[:, cols] = s0[:, cols] * gcol + upd[:, cols]


def _gla_rec_kernel(q_ref, k_ref, v_ref, og_ref, gl_ref, wgu_ref, gb_ref, nw_ref, tril_ref,
                    o_ref, sout_ref, s_ref, qs_ref, bc_ref, a_ref, *, blocks):
    l = pl.program_id(2)

    @pl.when(l == 0)
    def _():
        s_ref[...] = jnp.zeros_like(s_ref)

    def body(i, carry):
        _gla_block(pl.multiple_of(i * GLA_BLOCK, GLA_BLOCK), q_ref, k_ref, v_ref, og_ref, gl_ref, wgu_ref,
                   gb_ref, nw_ref, tril_ref, o_ref, s_ref, qs_ref, bc_ref, a_ref)
        return carry

    lax.fori_loop(0, blocks, body, 0)

    @pl.when(l == pl.num_programs(2) - 1)
    def _():
        sout_ref[...] = s_ref[...]


def _chunk_tril():
    i = jnp.arange(GLA_BLOCK)[:, None]
    j = jnp.arange(GLA_BLOCK)[None, :]
    return ((i // GLA_CHUNK == j // GLA_CHUNK) & (j <= i)).astype(BF16)


def _gla_rec_prompt(proj, batch, seq, wgu_pad, gate_bias, norm_w):
    tl = min(512, seq)
    nl = seq // tl
    kq = GLA_KEY_DIM // GLA_DK
    kv = 2 * GLA_KEY_DIM // GLA_DV
    row = lambda b, h, l: b * nl + l
    o, s_final = pl.pallas_call(
        functools.partial(_gla_rec_kernel, blocks=tl // GLA_BLOCK),
        grid=(batch, GLA_HEADS, nl),
        in_specs=[
            pl.BlockSpec((tl, GLA_DK), lambda b, h, l: (row(b, h, l), h)),
            pl.BlockSpec((tl, GLA_DK), lambda b, h, l: (row(b, h, l), kq + h)),
            pl.BlockSpec((tl, GLA_DV), lambda b, h, l: (row(b, h, l), kv + h)),
            pl.BlockSpec((tl, GLA_DV), lambda b, h, l: (row(b, h, l), kv + GLA_HEADS + h)),
            pl.BlockSpec((tl, LANES), lambda b, h, l: (row(b, h, l), PROJ_QKVG // LANES)),
            pl.BlockSpec((LANES, GLA_DK), lambda b, h, l: (0, h)),
            pl.BlockSpec((1, GLA_DK), lambda b, h, l: (0, h)),
            pl.BlockSpec((1, GLA_DV), lambda b, h, l: (0, 0)),
            pl.BlockSpec((GLA_BLOCK, GLA_BLOCK), lambda b, h, l: (0, 0)),
        ],
        out_specs=[
            pl.BlockSpec((tl, GLA_DV), lambda b, h, l: (row(b, h, l), h)),
            pl.BlockSpec((None, None, GLA_DK, GLA_DV), lambda b, h, l: (b, h, 0, 0)),
        ],
        out_shape=[
            jax.ShapeDtypeStruct((batch * seq, GLA_VAL_DIM), BF16),
            jax.ShapeDtypeStruct((batch, GLA_HEADS, GLA_DK, GLA_DV), F32),
        ],
        scratch_shapes=[
            pltpu.VMEM((GLA_DK, GLA_DV), F32),
            pltpu.VMEM((GLA_BLOCK, GLA_DK), F32),
            pltpu.VMEM((GLA_BLOCK, GLA_DK), F32),
            pltpu.VMEM((GLA_BLOCK, GLA_BLOCK), F32),
        ],
        compiler_params=_params("parallel", "parallel", "arbitrary"),
        name="gla_rec_prompt",
    )(proj, proj, proj, proj, proj, wgu_pad, gate_bias, norm_w, _chunk_tril())
    return o, s_final


def _gla_sample_kernel(q_ref, k_ref, v_ref, og_ref, gl_ref, s_ref, wgu_ref, gb_ref, nw_ref, *rest, tb):
    o_ref, sout_ref = rest[-2:]
    gk = jnp.dot(gl_ref[...].astype(BF16), wgu_ref[...], preferred_element_type=F32) + gb_ref[...]
    a = jnp.exp(_log_sigmoid(gk) / GLA_GATE_TEMP)
    q = q_ref[...] * (GLA_DK ** -0.5)
    k = k_ref[...]
    v = v_ref[...]
    qk = jnp.sum(q * k, axis=1, keepdims=True)
    pad = jnp.zeros((LANES - 3 * tb, GLA_DK), F32)
    cols = jnp.transpose(jnp.concatenate([a, k, q * a, pad], axis=0))
    outs = []
    for b in range(tb):
        s0 = s_ref[b]
        vb = v[b:b + 1]
        outs.append(qk[b:b + 1] * vb + jnp.sum(cols[:, 2 * tb + b:2 * tb + b + 1] * s0, axis=0, keepdims=True))
        sout_ref[b] = s0 * cols[:, b:b + 1] + cols[:, tb + b:tb + b + 1] * vb
    o_ref[...] = _rms_gate(jnp.concatenate(outs, axis=0), og_ref[...], nw_ref[...])


def _gla_rec_sample(proj, state_gla, new_state, layer, wgu_pad, gate_bias, norm_w):
    bs = proj.shape[0]
    tb = 8
    kq = GLA_KEY_DIM // GLA_DK
    kv = 2 * GLA_KEY_DIM // GLA_DV
    s_spec = pl.BlockSpec((None, tb, None, GLA_DK, GLA_DV), lambda i, h: (layer, i, h, 0, 0))
    in_specs = [
        pl.BlockSpec((tb, GLA_DK), lambda i, h: (i, h)),
        pl.BlockSpec((tb, GLA_DK), lambda i, h: (i, kq + h)),
        pl.BlockSpec((tb, GLA_DV), lambda i, h: (i, kv + h)),
        pl.BlockSpec((tb, GLA_DV), lambda i, h: (i, kv + GLA_HEADS + h)),
        pl.BlockSpec((tb, LANES), lambda i, h: (i, PROJ_QKVG // LANES)),
        s_spec,
        pl.BlockSpec((LANES, GLA_DK), lambda i, h: (0, h)),
        pl.BlockSpec((1, GLA_DK), lambda i, h: (0, h)),
        pl.BlockSpec((1, GLA_DV), lambda i, h: (0, 0)),
    ]
    args = [proj, proj, proj, proj, proj, state_gla, wgu_pad, gate_bias, norm_w]
    aliases = {}
    if new_state is not None:
        in_specs.append(pl.BlockSpec(memory_space=pl.ANY))
        args.append(new_state)
        aliases = {len(args) - 1: 1}
    return pl.pallas_call(
        functools.partial(_gla_sample_kernel, tb=tb),
        grid=(bs // tb, GLA_HEADS),
        in_specs=in_specs,
        out_specs=[pl.BlockSpec((tb, GLA_DV), lambda i, h: (i, h)), s_spec],
        out_shape=[
            jax.ShapeDtypeStruct((bs, GLA_VAL_DIM), F32),
            jax.ShapeDtypeStruct(state_gla.shape, F32),
        ],
        input_output_aliases=aliases,
        compiler_params=_params("parallel", "parallel"),
        name="gla_rec_sample",
    )(*args)


def _out_ln_kernel(o_ref, w_ref, x_ref, g_ref, b_ref, y_ref):
    y = jnp.dot(o_ref[...].astype(BF16), w_ref[...], preferred_element_type=F32)
    y_ref[...] = _layer_norm(DN_ALPHA * x_ref[...] + y, g_ref[...], b_ref[...])


def _out_ln(o2d, w_out, x2d, g, b):
    m = x2d.shape[0]
    tm = min(512, m)
    const = lambda *shape: pl.BlockSpec(shape, lambda i: (0,) * len(shape))
    return pl.pallas_call(
        _out_ln_kernel,
        grid=(m // tm,),
        in_specs=[
            pl.BlockSpec((tm, GLA_VAL_DIM), lambda i: (i, 0)),
            const(GLA_VAL_DIM, D_MODEL),
            pl.BlockSpec((tm, D_MODEL), lambda i: (i, 0)),
            const(1, D_MODEL), const(1, D_MODEL),
        ],
        out_specs=pl.BlockSpec((tm, D_MODEL), lambda i: (i, 0)),
        out_shape=jax.ShapeDtypeStruct(x2d.shape, F32),
        compiler_params=_params("parallel"),
        name="gla_out_ln",
    )(o2d, w_out, x2d, g, b)


def kernel(x_prompt, x_sample, state_pool, state_gla, pool_w, pool_scale, gla_w_in, gla_w_gate_up,
           gla_gate_bias, gla_norm_w, gla_w_out, ln_mix_g, ln_mix_b, mlp_w1, mlp_b1, mlp_w2, mlp_b2,
           ln_ffn_g, ln_ffn_b):
    batch, seq, _ = x_prompt.shape
    bs = x_sample.shape[0]
    xp = x_prompt.reshape(batch * seq, D_MODEL)
    xs = x_sample.reshape(bs, D_MODEL)
    state_flat = state_pool.reshape(state_pool.shape[0], bs, POOL_CTX * D_MODEL)
    row = lambda a, i: a[i].reshape(1, -1)

    pool_states_p, pool_states_s, gla_states_p = [], [], []
    new_gla_sample = None
    for i in range(DEPTH):
        j = i // 2
        g_mix, b_mix = row(ln_mix_g, i), row(ln_mix_b, i)
        if i % 2 == 0:
            pool_states_p.append(xp.reshape(batch, seq, D_MODEL)[:, seq - POOL_CTX:])
            pool_states_s.append(jnp.concatenate([state_pool[j][:, 1:], xs[:, None, :]], axis=1))
            w = pool_w[j].astype(BF16)
            scale = row(pool_scale, j)
            xp = _pool_ln_prompt(xp, batch, seq, w, scale, g_mix, b_mix)
            xs = _pool_ln_sample(xs, state_flat, j, w, scale, g_mix, b_mix)
        else:
            w_in = jnp.pad(gla_w_in[j].astype(BF16), ((0, 0), (0, PROJ_PAD - gla_w_in.shape[-1])))
            wgu = jnp.pad(gla_w_gate_up[j].astype(BF16), ((0, LANES - GLA_GATE_RANK), (0, 0)))
            gb, nw = row(gla_gate_bias, j), row(gla_norm_w, j)
            w_out = gla_w_out[j].astype(BF16)
            o_p, s_p = _gla_rec_prompt(_gla_proj(xp, w_in), batch, seq, wgu, gb, nw)
            gla_states_p.append(s_p)
            o_s, new_gla_sample = _gla_rec_sample(_gla_proj(xs, w_in), state_gla, new_gla_sample, j, wgu, gb, nw)
            xp = _out_ln(o_p, w_out, xp, g_mix, b_mix)
            xs = _out_ln(o_s, w_out, xs, g_mix, b_mix)
        w1, w2 = mlp_w1[i].astype(BF16), mlp_w2[i].astype(BF16)
        b1, b2 = row(mlp_b1, i), row(mlp_b2, i)
        g_ffn, b_ffn = row(ln_ffn_g, i), row(ln_ffn_b, i)
        xp = _mlp_ln(xp, w1, b1, w2, b2, g_ffn, b_ffn)
        xs = _mlp_ln(xs, w1, b1, w2, b2, g_ffn, b_ffn)

    return (xp.reshape(batch, seq, D_MODEL), xs.reshape(bs, 1, D_MODEL), jnp.stack(pool_states_p),
            jnp.stack(gla_states_p), jnp.stack(pool_states_s), new_gla_sample)
```

```python
import functools

import jax
import jax.numpy as jnp
from jax import lax
from jax.experimental import pallas as pl
from jax.experimental.pallas import tpu as pltpu

F32 = jnp.float32
BF16 = jnp.bfloat16

D_MODEL = 2048
DEPTH = 4
PAST_LEN = 16384
POOL_WINDOWS = (2, 4, 8, 16)
POOL_GROUP = D_MODEL // len(POOL_WINDOWS)
POOL_CTX = max(POOL_WINDOWS) - 1
GLA_HEADS = 4
GLA_KEY_DIM = D_MODEL // 2
GLA_VAL_DIM = D_MODEL
GLA_DK = GLA_KEY_DIM // GLA_HEADS
GLA_DV = GLA_VAL_DIM // GLA_HEADS
GLA_GATE_RANK = 16
GLA_GATE_TEMP = 16.0
GLA_CHUNK = 16
D_FF = 4 * D_MODEL
LN_EPS = 1e-5
RMS_EPS = 1e-5
DN_ALPHA = (2 * DEPTH) ** 0.25
LOG2_E = 1.4426950408889634

LANES = 128
SUBLANES = 8
MXU_DIM = 256
VMEM_LIMIT_BYTES = 56 * 1024 * 1024

PROJ_QKVG = 2 * GLA_KEY_DIM + 2 * GLA_VAL_DIM
PROJ_TILE = 5 * MXU_DIM
PROJ_PAD = -(-(PROJ_QKVG + LANES) // PROJ_TILE) * PROJ_TILE
GLA_BLOCK = 128
CHUNKS_PER_BLOCK = GLA_BLOCK // GLA_CHUNK
POOL_ROWS = 256
MLP_ROWS = 256


def _params(*sem):
    return pltpu.CompilerParams(dimension_semantics=sem, vmem_limit_bytes=VMEM_LIMIT_BYTES)


def _layer_spec(layer, *block):
    return pl.BlockSpec((None,) + block, lambda *_: (layer,) + (0,) * len(block))


def _layer_norm(v, g, b):
    mu = jnp.mean(v, axis=-1, keepdims=True)
    c = v - mu
    var = jnp.mean(c * c, axis=-1, keepdims=True)
    return c * lax.rsqrt(var + LN_EPS) * g + b


def _split_bf16(z, parts):
    out = []
    for _ in range(parts - 1):
        hi = z.astype(BF16)
        out.append(hi)
        z = z - hi.astype(F32)
    out.append(z.astype(BF16))
    return out


def _mask_dot(mask_bf16, z, parts):
    return sum(jnp.dot(mask_bf16, t, preferred_element_type=F32) for t in _split_bf16(z, parts))


def _log_sigmoid(x):
    return jnp.minimum(x, 0.0) - jnp.log(1.0 + jnp.exp(-jnp.abs(x)))


def _dot_nt(a, b):
    return lax.dot_general(a, b, (((1,), (1,)), ((), ())), preferred_element_type=F32)


def _dot_tn(a, b):
    return lax.dot_general(a, b, (((0,), (0,)), ((), ())), preferred_element_type=F32)


def _pool_ln_kernel(x_ref, halo_ref, bm_ref, bh_ref, w_ref, scale_ref, g_ref, b_ref, o_ref, *, tile_rows):
    j = pl.program_id(1)
    rb = min(POOL_ROWS, tile_rows)
    row = lax.broadcasted_iota(jnp.int32, (rb, 1), 0)
    for g, w in enumerate(POOL_WINDOWS):
        cols = slice(g * POOL_GROUP, (g + 1) * POOL_GROUP)
        for r in range(tile_rows // rb):
            xb = x_ref[r * rb:(r + 1) * rb, cols]
            if r == 0:
                halo = jnp.where(j == 0, 0.0, halo_ref[:, cols])
            else:
                halo = x_ref[r * rb - 16:r * rb, cols]
            win = _mask_dot(bm_ref[g], xb, 2)
            head = win[:16] + _mask_dot(bh_ref[g], halo, 2)
            win = jnp.concatenate([head, win[16:]], axis=0)
            count = jnp.minimum(j * tile_rows + r * rb + row + 1, w).astype(F32)
            p = win / count - xb
            y = jnp.dot(p.astype(BF16), w_ref[g], preferred_element_type=F32)
            o_ref[r * rb:(r + 1) * rb, cols] = y * scale_ref[:, cols]
    ln_rows = min(128, tile_rows)
    for r in range(tile_rows // ln_rows):
        rows = slice(r * ln_rows, (r + 1) * ln_rows)
        v = DN_ALPHA * x_ref[rows, :] + o_ref[rows, :]
        o_ref[rows, :] = _layer_norm(v, g_ref[...], b_ref[...])


def _band_matrices(rb):
    i = jnp.arange(rb)[:, None]
    jj = jnp.arange(rb)[None, :]
    bm = jnp.stack([((i - jj >= 0) & (i - jj < w)) for w in POOL_WINDOWS]).astype(BF16)
    i16 = jnp.arange(16)[:, None]
    j16 = jnp.arange(16)[None, :]
    bh = jnp.stack([(j16 >= i16 + 17 - w) for w in POOL_WINDOWS]).astype(BF16)
    return bm, bh


def _pool_ln_prompt(x2d, batch, seq, layer, mix_layer, w_all, scale_all, g_all, b_all):
    tile_rows = min(1024, seq)
    nt = seq // tile_rows
    bm, bh = _band_matrices(min(POOL_ROWS, tile_rows))
    const = lambda *shape: pl.BlockSpec(shape, lambda bi, j: (0,) * len(shape))
    return pl.pallas_call(
        functools.partial(_pool_ln_kernel, tile_rows=tile_rows),
        grid=(batch, nt),
        in_specs=[
            pl.BlockSpec((tile_rows, D_MODEL), lambda bi, j: (bi * nt + j, 0)),
            pl.BlockSpec((16, D_MODEL), lambda bi, j: (jnp.maximum((bi * nt + j) * (tile_rows // 16) - 1, 0), 0)),
            const(*bm.shape), const(*bh.shape),
            _layer_spec(layer, *w_all.shape[1:]),
            _layer_spec(layer, 1, D_MODEL), _layer_spec(mix_layer, 1, D_MODEL), _layer_spec(mix_layer, 1, D_MODEL),
        ],
        out_specs=pl.BlockSpec((tile_rows, D_MODEL), lambda bi, j: (bi * nt + j, 0)),
        out_shape=jax.ShapeDtypeStruct(x2d.shape, F32),
        compiler_params=_params("parallel", "arbitrary"),
        name="pool_ln_prompt",
    )(x2d, x2d, bm, bh, w_all, scale_all, g_all, b_all)


def _pool_ln_sample_kernel(x_ref, st_ref, w_ref, scale_ref, g_ref, b_ref, o_ref, win_ref):
    row = lax.broadcasted_iota(jnp.int32, (POOL_CTX, POOL_GROUP), 0)
    for bi in range(x_ref.shape[0]):
        for g, w in enumerate(POOL_WINDOWS):
            cols = slice(g * POOL_GROUP, (g + 1) * POOL_GROUP)
            carried = jnp.where(row >= POOL_CTX - (w - 1), st_ref[bi, :, cols], 0.0)
            win_ref[bi:bi + 1, cols] = jnp.sum(carried, axis=0, keepdims=True)
    x = x_ref[...]
    for g, w in enumerate(POOL_WINDOWS):
        cols = slice(g * POOL_GROUP, (g + 1) * POOL_GROUP)
        xg = x[:, cols]
        p = (win_ref[:, cols] + xg) / float(min(PAST_LEN + 1, w)) - xg
        o_ref[:, cols] = jnp.dot(p.astype(BF16), w_ref[g], preferred_element_type=F32) * scale_ref[:, cols]
    o_ref[...] = _layer_norm(DN_ALPHA * x + o_ref[...], g_ref[...], b_ref[...])


def _pool_ln_sample(xs, state_pool, layer, mix_layer, w_all, scale_all, g_all, b_all):
    bs = xs.shape[0]
    tb = min(32, bs)
    return pl.pallas_call(
        _pool_ln_sample_kernel,
        grid=(bs // tb,),
        in_specs=[
            pl.BlockSpec((tb, D_MODEL), lambda i: (i, 0)),
            pl.BlockSpec((None, tb, POOL_CTX, D_MODEL), lambda i: (layer, i, 0, 0)),
            _layer_spec(layer, *w_all.shape[1:]),
            _layer_spec(layer, 1, D_MODEL), _layer_spec(mix_layer, 1, D_MODEL), _layer_spec(mix_layer, 1, D_MODEL),
        ],
        out_specs=pl.BlockSpec((tb, D_MODEL), lambda i: (i, 0)),
        out_shape=jax.ShapeDtypeStruct(xs.shape, F32),
        scratch_shapes=[pltpu.VMEM((tb, D_MODEL), F32)],
        compiler_params=_params("parallel"),
        name="pool_ln_sample",
    )(xs, state_pool, w_all, scale_all, g_all, b_all)


def _mlp_ln_kernel(x_ref, w1_ref, b1_ref, w2_ref, b2_ref, g_ref, b_ref, o_ref, xb_ref):
    f = pl.program_id(1)

    @pl.when(f == 0)
    def _():
        xb_ref[...] = x_ref[...].astype(BF16)
        o_ref[...] = jnp.zeros_like(o_ref)

    tm = x_ref.shape[0]
    rb = min(MLP_ROWS, tm)
    for r in range(tm // rb):
        rows = slice(r * rb, (r + 1) * rb)
        h = jnp.dot(xb_ref[rows, :], w1_ref[...], preferred_element_type=F32) + b1_ref[...]
        h = jnp.square(jnp.maximum(h, 0.0))
        o_ref[rows, :] += jnp.dot(h.astype(BF16), w2_ref[...], preferred_element_type=F32)

    @pl.when(f == pl.num_programs(1) - 1)
    def _():
        for r in range(tm // rb):
            rows = slice(r * rb, (r + 1) * rb)
            v = DN_ALPHA * x_ref[rows, :] + (o_ref[rows, :] + b2_ref[...])
            o_ref[rows, :] = _layer_norm(v, g_ref[...], b_ref[...])


def _mlp_ln(x2d, layer, w1_all, b1_all, w2_all, b2_all, g_all, b_all):
    m = x2d.shape[0]
    tm = min(1024, m)
    tf = 512
    vec = lambda: pl.BlockSpec((None, 1, D_MODEL), lambda i, f: (layer, 0, 0))
    return pl.pallas_call(
        _mlp_ln_kernel,
        grid=(m // tm, D_FF // tf),
        in_specs=[
            pl.BlockSpec((tm, D_MODEL), lambda i, f: (i, 0)),
            pl.BlockSpec((None, D_MODEL, tf), lambda i, f: (layer, 0, f)),
            pl.BlockSpec((None, 1, tf), lambda i, f: (layer, 0, f)),
            pl.BlockSpec((None, tf, D_MODEL), lambda i, f: (layer, f, 0)),
            vec(), vec(), vec(),
        ],
        out_specs=pl.BlockSpec((tm, D_MODEL), lambda i, f: (i, 0)),
        out_shape=jax.ShapeDtypeStruct(x2d.shape, F32),
        scratch_shapes=[pltpu.VMEM((tm, D_MODEL), BF16)],
        compiler_params=_params("parallel", "arbitrary"),
        name="mlp_ln",
    )(x2d, w1_all, b1_all, w2_all, b2_all, g_all, b_all)


def _proj_kernel(x_ref, w_ref, o_ref, xb_ref):
    @pl.when(pl.program_id(1) == 0)
    def _():
        xb_ref[...] = x_ref[...].astype(BF16)

    o_ref[...] = jnp.dot(xb_ref[...], w_ref[...], preferred_element_type=F32)


def _gla_proj(x2d, layer, w_in_all):
    m = x2d.shape[0]
    tm = min(1024, m)
    return pl.pallas_call(
        _proj_kernel,
        grid=(m // tm, PROJ_PAD // PROJ_TILE),
        in_specs=[
            pl.BlockSpec((tm, D_MODEL), lambda i, n: (i, 0)),
            pl.BlockSpec((None, D_MODEL, PROJ_TILE), lambda i, n: (layer, 0, n)),
        ],
        out_specs=pl.BlockSpec((tm, PROJ_TILE), lambda i, n: (i, n)),
        out_shape=jax.ShapeDtypeStruct((m, PROJ_PAD), F32),
        scratch_shapes=[pltpu.VMEM((tm, D_MODEL), BF16)],
        compiler_params=_params("parallel", "arbitrary"),
        name="gla_proj",
    )(x2d, w_in_all)


def _rms_gate(o, og, nw):
    o = o * lax.rsqrt(jnp.mean(o * o, axis=-1, keepdims=True) + RMS_EPS) * nw
    return o * (og * jax.nn.sigmoid(og))


def _gla_block(r0, q_ref, k_ref, v_ref, og_ref, nw_ref, o_ref, s_ref, bc_ref, a_ref):
    scale = GLA_DK ** -0.5
    half = GLA_CHUNK // 2
    chunks = range(CHUNKS_PER_BLOCK)
    blk = slice(r0, r0 + GLA_BLOCK)
    a_blk = a_ref.at[blk]

    tot = [bc_ref[r0 + (c + 1) * GLA_CHUNK - 1:r0 + (c + 1) * GLA_CHUNK, :] for c in chunks]
    before = [jnp.zeros_like(tot[0])]
    for c in chunks[1:]:
        before.append(before[-1] + tot[c - 1])
    after = [jnp.zeros_like(tot[0])]
    for c in reversed(chunks[:-1]):
        after.insert(0, after[0] + tot[c + 1])
    total = before[-1] + tot[-1]

    v = v_ref[blk, :].astype(BF16)
    qs, ks, bcs, qd, kd, q_hi, k_lo = [], [], [], [], [], [], []
    for c in chunks:
        rows = slice(r0 + c * GLA_CHUNK, r0 + (c + 1) * GLA_CHUNK)
        bcc = bc_ref[rows, :]
        qc = q_ref[rows, :] * scale
        kc = k_ref[rows, :]
        mid = bcc[half:half + 1]
        qs.append(qc)
        ks.append(kc)
        bcs.append(bcc)
        qd.append(qc * jnp.exp2(bcc))
        kd.append(kc * jnp.exp2(tot[c] - bcc))
        q_hi.append(qc[half:] * jnp.exp2(bcc[half:] - mid))
        k_lo.append(kc[:half] * jnp.exp2(mid - bcc[:half]))

    s0 = s_ref[...]
    q_in = jnp.concatenate([qd[c] * jnp.exp2(before[c]) for c in chunks], axis=0)
    o = jnp.dot(q_in.astype(BF16), s0.astype(BF16), preferred_element_type=F32)

    kd_all = jnp.concatenate(kd, axis=0).astype(BF16)
    lhs = []
    for jc in chunks[:-1]:
        lhs.append(qd[jc + 1])
        lhs.extend(qd[c] * jnp.exp2(before[c] - before[jc + 1]) for c in chunks[jc + 2:])
    pair = _dot_nt(jnp.concatenate(lhs, axis=0).astype(BF16), kd_all)
    a_blk[...] = jnp.zeros((GLA_BLOCK, GLA_BLOCK), F32)
    off = 0
    for jc in chunks[:-1]:
        lo = (jc + 1) * GLA_CHUNK
        n = GLA_BLOCK - lo
        lane = lax.broadcasted_iota(jnp.int32, (n, GLA_BLOCK), 1)
        keep = (lane >= jc * GLA_CHUNK) & (lane < lo)
        a_blk[lo:, :] += jnp.where(keep, pair[off:off + n], 0.0)
        off += n

    zeros = jnp.zeros((half, GLA_DK), F32)
    k_lo_all = jnp.concatenate([t for c in chunks for t in (k_lo[c], zeros)], axis=0).astype(BF16)
    cross = _dot_nt(jnp.concatenate(q_hi, axis=0).astype(BF16), k_lo_all)
    row = lax.broadcasted_iota(jnp.int32, (half, GLA_DK), 0)
    lane = lax.broadcasted_iota(jnp.int32, (half, GLA_BLOCK), 1)
    for c in chunks:
        for hs in range(2):
            rs = slice(hs * half, (hs + 1) * half)
            qh, kh, bh = qs[c][rs], ks[c][rs], bcs[c][rs]
            col0 = c * GLA_CHUNK + hs * half
            acc = jnp.zeros((half, GLA_BLOCK), F32)
            if hs == 1:
                in_chunk = (lane >= c * GLA_CHUNK) & (lane < col0)
                acc = jnp.where(in_chunk, cross[c * half:(c + 1) * half], 0.0)
            for j in range(half):
                d = bh - bh[j:j + 1]
                e = jnp.exp2(jnp.where(row >= j, d, -jnp.inf)) if j else jnp.exp2(d)
                t = qh * e * kh[j:j + 1]
                acc = acc + jnp.where(lane == col0 + j, jnp.sum(t, axis=1, keepdims=True), 0.0)
            a_blk[col0:col0 + half, :] += acc

    o = o + jnp.dot(a_blk[...].astype(BF16), v, preferred_element_type=F32)
    o_ref[blk, :] = _rms_gate(o, og_ref[blk, :], nw_ref[...]).astype(o_ref.dtype)

    k_out = jnp.concatenate([kd[c] * jnp.exp2(after[c]) for c in chunks], axis=0)
    upd = _dot_tn(k_out.astype(BF16), v)
    gcol = jnp.transpose(jnp.broadcast_to(jnp.exp2(total), (LANES, GLA_DK)))
    for n in range(GLA_DV // LANES):
        cols = slice(n * LANES, (n + 1) * LANES)
        s_ref[:, cols] = s0[:, cols] * gcol + upd[:, cols]


def _gla_rec_kernel(q_ref, k_ref, v_ref, og_ref, gl_ref, wgu_ref, gb_ref, nw_ref, tril_ref,
                    o_ref, sout_ref, s_ref, bc_ref, a_ref):
    l = pl.program_id(2)

    @pl.when(l == 0)
    def _():
        s_ref[...] = jnp.zeros_like(s_ref)

    gk = jnp.dot(gl_ref[...].astype(BF16), wgu_ref[...], preferred_element_type=F32) + gb_ref[...]
    la = _log_sigmoid(gk) * (LOG2_E / GLA_GATE_TEMP)
    blocks = q_ref.shape[0] // GLA_BLOCK
    for i in range(blocks):
        blk = slice(i * GLA_BLOCK, (i + 1) * GLA_BLOCK)
        bc_ref[blk, :] = _mask_dot(tril_ref[...], la[blk], 2)
    for i in range(blocks):
        _gla_block(i * GLA_BLOCK, q_ref, k_ref, v_ref, og_ref, nw_ref, o_ref, s_ref, bc_ref, a_ref)

    @pl.when(l == pl.num_programs(2) - 1)
    def _():
        sout_ref[...] = s_ref[...]


def _chunk_tril():
    i = jnp.arange(GLA_BLOCK)[:, None]
    j = jnp.arange(GLA_BLOCK)[None, :]
    return ((i // GLA_CHUNK == j // GLA_CHUNK) & (j <= i)).astype(BF16)


def _gla_rec_prompt(proj, batch, seq, layer, wgu_all, gb_all, nw_all):
    tl = min(512, seq)
    nl = seq // tl
    kq = GLA_KEY_DIM // GLA_DK
    kv = 2 * GLA_KEY_DIM // GLA_DV
    row = lambda b, h, l: b * nl + l
    o, s_final = pl.pallas_call(
        _gla_rec_kernel,
        grid=(batch, GLA_HEADS, nl),
        in_specs=[
            pl.BlockSpec((tl, GLA_DK), lambda b, h, l: (row(b, h, l), h)),
            pl.BlockSpec((tl, GLA_DK), lambda b, h, l: (row(b, h, l), kq + h)),
            pl.BlockSpec((tl, GLA_DV), lambda b, h, l: (row(b, h, l), kv + h)),
            pl.BlockSpec((tl, GLA_DV), lambda b, h, l: (row(b, h, l), kv + GLA_HEADS + h)),
            pl.BlockSpec((tl, LANES), lambda b, h, l: (row(b, h, l), PROJ_QKVG // LANES)),
            pl.BlockSpec((None, LANES, GLA_DK), lambda b, h, l: (layer, 0, h)),
            pl.BlockSpec((None, 1, GLA_DK), lambda b, h, l: (layer, 0, h)),
            _layer_spec(layer, 1, GLA_DV),
            pl.BlockSpec((GLA_BLOCK, GLA_BLOCK), lambda b, h, l: (0, 0)),
        ],
        out_specs=[
            pl.BlockSpec((tl, GLA_DV), lambda b, h, l: (row(b, h, l), h)),
            pl.BlockSpec((None, None, GLA_DK, GLA_DV), lambda b, h, l: (b, h, 0, 0)),
        ],
        out_shape=[
            jax.ShapeDtypeStruct((batch * seq, GLA_VAL_DIM), BF16),
            jax.ShapeDtypeStruct((batch, GLA_HEADS, GLA_DK, GLA_DV), F32),
        ],
        scratch_shapes=[
            pltpu.VMEM((GLA_DK, GLA_DV), F32),
            pltpu.VMEM((tl, GLA_DK), F32),
            pltpu.VMEM((tl, GLA_BLOCK), F32),
        ],
        compiler_params=_params("parallel", "parallel", "arbitrary"),
        name="gla_rec_prompt",
    )(proj, proj, proj, proj, proj, wgu_all, gb_all, nw_all, _chunk_tril())
    return o, s_final


def _gla_sample_kernel(q_ref, k_ref, v_ref, og_ref, gl_ref, s_ref, wgu_ref, gb_ref, nw_ref, *rest, tb):
    o_ref, sout_ref = rest[-2:]
    gk = jnp.dot(gl_ref[...].astype(BF16), wgu_ref[...], preferred_element_type=F32) + gb_ref[...]
    a = jnp.exp(_log_sigmoid(gk) / GLA_GATE_TEMP)
    q = q_ref[...] * (GLA_DK ** -0.5)
    k = k_ref[...]
    v = v_ref[...]
    qk = jnp.sum(q * k, axis=1, keepdims=True)
    pad = jnp.zeros((LANES - 3 * tb, GLA_DK), F32)
    cols = jnp.transpose(jnp.concatenate([a, k, q * a, pad], axis=0))
    outs = []
    for b in range(tb):
        s0 = s_ref[b]
        vb = v[b:b + 1]
        outs.append(qk[b:b + 1] * vb + jnp.sum(cols[:, 2 * tb + b:2 * tb + b + 1] * s0, axis=0, keepdims=True))
        sout_ref[b] = s0 * cols[:, b:b + 1] + cols[:, tb + b:tb + b + 1] * vb
    o_ref[...] = _rms_gate(jnp.concatenate(outs, axis=0), og_ref[...], nw_ref[...])


def _gla_rec_sample(proj, state_gla, new_state, layer, wgu_all, gb_all, nw_all):
    bs = proj.shape[0]
    tb = SUBLANES
    kq = GLA_KEY_DIM // GLA_DK
    kv = 2 * GLA_KEY_DIM // GLA_DV
    s_spec = pl.BlockSpec((None, tb, None, GLA_DK, GLA_DV), lambda i, h: (layer, i, h, 0, 0))
    in_specs = [
        pl.BlockSpec((tb, GLA_DK), lambda i, h: (i, h)),
        pl.BlockSpec((tb, GLA_DK), lambda i, h: (i, kq + h)),
        pl.BlockSpec((tb, GLA_DV), lambda i, h: (i, kv + h)),
        pl.BlockSpec((tb, GLA_DV), lambda i, h: (i, kv + GLA_HEADS + h)),
        pl.BlockSpec((tb, LANES), lambda i, h: (i, PROJ_QKVG // LANES)),
        s_spec,
        pl.BlockSpec((None, LANES, GLA_DK), lambda i, h: (layer, 0, h)),
        pl.BlockSpec((None, 1, GLA_DK), lambda i, h: (layer, 0, h)),
        _layer_spec(layer, 1, GLA_DV),
    ]
    args = [proj, proj, proj, proj, proj, state_gla, wgu_all, gb_all, nw_all]
    aliases = {}
    if new_state is not None:
        in_specs.append(pl.BlockSpec(memory_space=pl.ANY))
        args.append(new_state)
        aliases = {len(args) - 1: 1}
    return pl.pallas_call(
        functools.partial(_gla_sample_kernel, tb=tb),
        grid=(bs // tb, GLA_HEADS),
        in_specs=in_specs,
        out_specs=[pl.BlockSpec((tb, GLA_DV), lambda i, h: (i, h)), s_spec],
        out_shape=[
            jax.ShapeDtypeStruct((bs, GLA_VAL_DIM), F32),
            jax.ShapeDtypeStruct(state_gla.shape, F32),
        ],
        input_output_aliases=aliases,
        compiler_params=_params("parallel", "parallel"),
        name="gla_rec_sample",
    )(*args)


def _out_ln_kernel(o_ref, w_ref, x_ref, g_ref, b_ref, y_ref):
    y = jnp.dot(o_ref[...].astype(BF16), w_ref[...], preferred_element_type=F32)
    y_ref[...] = _layer_norm(DN_ALPHA * x_ref[...] + y, g_ref[...], b_ref[...])


def _out_ln(o2d, x2d, layer, mix_layer, w_out_all, g_all, b_all):
    m = x2d.shape[0]
    tm = min(512, m)
    return pl.pallas_call(
        _out_ln_kernel,
        grid=(m // tm,),
        in_specs=[
            pl.BlockSpec((tm, GLA_VAL_DIM), lambda i: (i, 0)),
            _layer_spec(layer, GLA_VAL_DIM, D_MODEL),
            pl.BlockSpec((tm, D_MODEL), lambda i: (i, 0)),
            _layer_spec(mix_layer, 1, D_MODEL), _layer_spec(mix_layer, 1, D_MODEL),
        ],
        out_specs=pl.BlockSpec((tm, D_MODEL), lambda i: (i, 0)),
        out_shape=jax.ShapeDtypeStruct(x2d.shape, F32),
        compiler_params=_params("parallel"),
        name="gla_out_ln",
    )(o2d, w_out_all, x2d, g_all, b_all)


def kernel(x_prompt, x_sample, state_pool, state_gla, pool_w, pool_scale, gla_w_in, gla_w_gate_up,
           gla_gate_bias, gla_norm_w, gla_w_out, ln_mix_g, ln_mix_b, mlp_w1, mlp_b1, mlp_w2, mlp_b2,
           ln_ffn_g, ln_ffn_b):
    batch, seq, _ = x_prompt.shape
    bs = x_sample.shape[0]
    xp = x_prompt.reshape(batch * seq, D_MODEL)
    xs = x_sample.reshape(bs, D_MODEL)

    vec = lambda a: a.reshape(a.shape[0], 1, a.shape[1])
    pool_w16, w1, w2, w_out = (a.astype(BF16) for a in (pool_w, mlp_w1, mlp_w2, gla_w_out))
    w_in = jnp.pad(gla_w_in.astype(BF16), ((0, 0), (0, 0), (0, PROJ_PAD - gla_w_in.shape[-1])))
    wgu = jnp.pad(gla_w_gate_up.astype(BF16), ((0, 0), (0, LANES - GLA_GATE_RANK), (0, 0)))
    pool_scale, gate_bias, norm_w, mix_g, mix_b, b1, b2, ffn_g, ffn_b = map(
        vec, (pool_scale, gla_gate_bias, gla_norm_w, ln_mix_g, ln_mix_b, mlp_b1, mlp_b2, ln_ffn_g, ln_ffn_b))

    pool_in_p, pool_in_s, gla_states_p = [], [], []
    new_gla_sample = None
    for i in range(DEPTH):
        j = i // 2
        if i % 2 == 0:
            pool_in_p.append(xp.reshape(batch, seq, D_MODEL)[:, seq - POOL_CTX:])
            pool_in_s.append(xs)
            xp = _pool_ln_prompt(xp, batch, seq, j, i, pool_w16, pool_scale, mix_g, mix_b)
            xs = _pool_ln_sample(xs, state_pool, j, i, pool_w16, pool_scale, mix_g, mix_b)
        else:
            o_p, s_p = _gla_rec_prompt(_gla_proj(xp, j, w_in), batch, seq, j, wgu, gate_bias, norm_w)
            gla_states_p.append(s_p)
            o_s, new_gla_sample = _gla_rec_sample(_gla_proj(xs, j, w_in), state_gla, new_gla_sample, j, wgu,
                                                  gate_bias, norm_w)
            xp = _out_ln(o_p, xp, j, i, w_out, mix_g, mix_b)
            xs = _out_ln(o_s, xs, j, i, w_out, mix_g, mix_b)
        xp = _mlp_ln(xp, i, w1, b1, w2, b2, ffn_g, ffn_b)
        xs = _mlp_ln(xs, i, w1, b1, w2, b2, ffn_g, ffn_b)

    new_pool_sample = jnp.concatenate([state_pool[:, :, 1:], jnp.stack(pool_in_s)[:, :, None, :]], axis=2)
    return (xp.reshape(batch, seq, D_MODEL), xs.reshape(bs, 1, D_MODEL), jnp.stack(pool_in_p),
            jnp.stack(gla_states_p), new_pool_sample, new_gla_sample)
```

```python
import functools

import jax
import jax.numpy as jnp
from jax import lax
from jax.experimental import pallas as pl
from jax.experimental.pallas import tpu as pltpu

F32 = jnp.float32
BF16 = jnp.bfloat16

D_MODEL = 2048
DEPTH = 4
PAST_LEN = 16384
POOL_WINDOWS = (2, 4, 8, 16)
POOL_GROUP = D_MODEL // len(POOL_WINDOWS)
POOL_CTX = max(POOL_WINDOWS) - 1
GLA_HEADS = 4
GLA_KEY_DIM = D_MODEL // 2
GLA_VAL_DIM = D_MODEL
GLA_DK = GLA_KEY_DIM // GLA_HEADS
GLA_DV = GLA_VAL_DIM // GLA_HEADS
GLA_GATE_RANK = 16
GLA_GATE_TEMP = 16.0
GLA_CHUNK = 16
D_FF = 4 * D_MODEL
LN_EPS = 1e-5
RMS_EPS = 1e-5
DN_ALPHA = (2 * DEPTH) ** 0.25
LOG2_E = 1.4426950408889634

LANES = 128
SUBLANES = 8
MXU_DIM = 256
VMEM_LIMIT_BYTES = 60 * 1024 * 1024

PROJ_QKVG = 2 * GLA_KEY_DIM + 2 * GLA_VAL_DIM
PROJ_TILE = 5 * MXU_DIM
PROJ_PAD = -(-(PROJ_QKVG + LANES) // PROJ_TILE) * PROJ_TILE
GLA_BLOCK = 128
CHUNKS_PER_BLOCK = GLA_BLOCK // GLA_CHUNK
POOL_ROWS = 256
MLP_ROWS = 256
OUT_ROWS = 128


def _params(*sem):
    return pltpu.CompilerParams(dimension_semantics=sem, vmem_limit_bytes=VMEM_LIMIT_BYTES)


def _layer_spec(layer, *block):
    return pl.BlockSpec((None,) + block, lambda *_: (layer,) + (0,) * len(block))


def _layer_norm(v, g, b):
    mu = jnp.mean(v, axis=-1, keepdims=True)
    c = v - mu
    var = jnp.mean(c * c, axis=-1, keepdims=True)
    return c * lax.rsqrt(var + LN_EPS) * g + b


def _split_bf16(z, parts):
    out = []
    for _ in range(parts - 1):
        hi = z.astype(BF16)
        out.append(hi)
        z = z - hi.astype(F32)
    out.append(z.astype(BF16))
    return out


def _mask_dot(mask_bf16, z, parts):
    return sum(jnp.dot(mask_bf16, t, preferred_element_type=F32) for t in _split_bf16(z, parts))


def _log_sigmoid(x):
    return jnp.minimum(x, 0.0) - jnp.log(1.0 + jnp.exp(-jnp.abs(x)))


def _dot_nt(a, b):
    return lax.dot_general(a, b, (((1,), (1,)), ((), ())), preferred_element_type=F32)


def _dot_tn(a, b):
    return lax.dot_general(a, b, (((0,), (0,)), ((), ())), preferred_element_type=F32)


def _pool_ln_kernel(x_ref, halo_ref, bm_ref, bh_ref, w_ref, scale_ref, g_ref, b_ref, o_ref, *, tile_rows):
    j = pl.program_id(1)
    rb = min(POOL_ROWS, tile_rows)
    row = lax.broadcasted_iota(jnp.int32, (rb, 1), 0)
    for g, w in enumerate(POOL_WINDOWS):
        cols = slice(g * POOL_GROUP, (g + 1) * POOL_GROUP)
        for r in range(tile_rows // rb):
            xb = x_ref[r * rb:(r + 1) * rb, cols]
            if r == 0:
                halo = jnp.where(j == 0, 0.0, halo_ref[:, cols])
            else:
                halo = x_ref[r * rb - 16:r * rb, cols]
            win = _mask_dot(bm_ref[g], xb, 2)
            head = win[:16] + _mask_dot(bh_ref[g], halo, 2)
            win = jnp.concatenate([head, win[16:]], axis=0)
            count = jnp.minimum(j * tile_rows + r * rb + row + 1, w).astype(F32)
            p = win / count - xb
            y = jnp.dot(p.astype(BF16), w_ref[g], preferred_element_type=F32)
            o_ref[r * rb:(r + 1) * rb, cols] = y * scale_ref[:, cols]
    ln_rows = min(128, tile_rows)
    for r in range(tile_rows // ln_rows):
        rows = slice(r * ln_rows, (r + 1) * ln_rows)
        v = DN_ALPHA * x_ref[rows, :] + o_ref[rows, :]
        o_ref[rows, :] = _layer_norm(v, g_ref[...], b_ref[...])


def _band_matrices(rb):
    i = jnp.arange(rb)[:, None]
    jj = jnp.arange(rb)[None, :]
    bm = jnp.stack([((i - jj >= 0) & (i - jj < w)) for w in POOL_WINDOWS]).astype(BF16)
    i16 = jnp.arange(16)[:, None]
    j16 = jnp.arange(16)[None, :]
    bh = jnp.stack([(j16 >= i16 + 17 - w) for w in POOL_WINDOWS]).astype(BF16)
    return bm, bh


def _pool_ln_prompt(x2d, batch, seq, layer, mix_layer, w_all, scale_all, g_all, b_all):
    tile_rows = min(1024, seq)
    nt = seq // tile_rows
    bm, bh = _band_matrices(min(POOL_ROWS, tile_rows))
    const = lambda *shape: pl.BlockSpec(shape, lambda bi, j: (0,) * len(shape))
    return pl.pallas_call(
        functools.partial(_pool_ln_kernel, tile_rows=tile_rows),
        grid=(batch, nt),
        in_specs=[
            pl.BlockSpec((tile_rows, D_MODEL), lambda bi, j: (bi * nt + j, 0)),
            pl.BlockSpec((16, D_MODEL), lambda bi, j: (jnp.maximum((bi * nt + j) * (tile_rows // 16) - 1, 0), 0)),
            const(*bm.shape), const(*bh.shape),
            _layer_spec(layer, *w_all.shape[1:]),
            _layer_spec(layer, 1, D_MODEL), _layer_spec(mix_layer, 1, D_MODEL), _layer_spec(mix_layer, 1, D_MODEL),
        ],
        out_specs=pl.BlockSpec((tile_rows, D_MODEL), lambda bi, j: (bi * nt + j, 0)),
        out_shape=jax.ShapeDtypeStruct(x2d.shape, F32),
        compiler_params=_params("parallel", "arbitrary"),
        name="pool_ln_prompt",
    )(x2d, x2d, bm, bh, w_all, scale_all, g_all, b_all)


def _pool_ln_sample_kernel(x_ref, st_ref, w_ref, scale_ref, g_ref, b_ref, o_ref):
    x = x_ref[...]
    for g, w in enumerate(POOL_WINDOWS):
        cols = slice(g * POOL_GROUP, (g + 1) * POOL_GROUP)
        xg = x[:, cols]
        win = xg
        for r in range(POOL_CTX - (w - 1), POOL_CTX):
            win = win + st_ref[r, :, cols]
        p = win / float(min(PAST_LEN + 1, w)) - xg
        o_ref[:, cols] = jnp.dot(p.astype(BF16), w_ref[g], preferred_element_type=F32) * scale_ref[:, cols]
    o_ref[...] = _layer_norm(DN_ALPHA * x + o_ref[...], g_ref[...], b_ref[...])


def _pool_ln_sample(xs, state_rows, layer, mix_layer, w_all, scale_all, g_all, b_all):
    bs = xs.shape[0]
    tb = min(32, bs)
    return pl.pallas_call(
        _pool_ln_sample_kernel,
        grid=(bs // tb,),
        in_specs=[
            pl.BlockSpec((tb, D_MODEL), lambda i: (i, 0)),
            pl.BlockSpec((None, POOL_CTX, tb, D_MODEL), lambda i: (layer, 0, i, 0)),
            _layer_spec(layer, *w_all.shape[1:]),
            _layer_spec(layer, 1, D_MODEL), _layer_spec(mix_layer, 1, D_MODEL), _layer_spec(mix_layer, 1, D_MODEL),
        ],
        out_specs=pl.BlockSpec((tb, D_MODEL), lambda i: (i, 0)),
        out_shape=jax.ShapeDtypeStruct(xs.shape, F32),
        compiler_params=_params("parallel"),
        name="pool_ln_sample",
    )(xs, state_rows, w_all, scale_all, g_all, b_all)


def _mlp_ln_kernel(x_ref, w1_ref, b1_ref, w2_ref, b2_ref, g_ref, b_ref, o_ref, xb_ref):
    f = pl.program_id(1)

    @pl.when(f == 0)
    def _():
        xb_ref[...] = x_ref[...].astype(BF16)
        o_ref[...] = jnp.zeros_like(o_ref)

    tm = x_ref.shape[0]
    rb = min(MLP_ROWS, tm)

    def accumulate(finish):
        for r in range(tm // rb):
            rows = slice(r * rb, (r + 1) * rb)
            h = jnp.dot(xb_ref[rows, :], w1_ref[...], preferred_element_type=F32) + b1_ref[...]
            h = jnp.square(jnp.maximum(h, 0.0))
            acc = o_ref[rows, :] + jnp.dot(h.astype(BF16), w2_ref[...], preferred_element_type=F32)
            if finish:
                v = DN_ALPHA * x_ref[rows, :] + (acc + b2_ref[...])
                acc = _layer_norm(v, g_ref[...], b_ref[...])
            o_ref[rows, :] = acc

    last = pl.num_programs(1) - 1
    pl.when(f < last)(functools.partial(accumulate, False))
    pl.when(f == last)(functools.partial(accumulate, True))


def _mlp_ln(x2d, layer, w1_all, b1_all, w2_all, b2_all, g_all, b_all):
    m = x2d.shape[0]
    tm = min(1024, m)
    tf = 1024
    vec = lambda: pl.BlockSpec((None, 1, D_MODEL), lambda i, f: (layer, 0, 0))
    return pl.pallas_call(
        _mlp_ln_kernel,
        grid=(m // tm, D_FF // tf),
        in_specs=[
            pl.BlockSpec((tm, D_MODEL), lambda i, f: (i, 0)),
            pl.BlockSpec((None, D_MODEL, tf), lambda i, f: (layer, 0, f)),
            pl.BlockSpec((None, 1, tf), lambda i, f: (layer, 0, f)),
            pl.BlockSpec((None, tf, D_MODEL), lambda i, f: (layer, f, 0)),
            vec(), vec(), vec(),
        ],
        out_specs=pl.BlockSpec((tm, D_MODEL), lambda i, f: (i, 0)),
        out_shape=jax.ShapeDtypeStruct(x2d.shape, F32),
        scratch_shapes=[pltpu.VMEM((tm, D_MODEL), BF16)],
        compiler_params=_params("parallel", "arbitrary"),
        name="mlp_ln",
    )(x2d, w1_all, b1_all, w2_all, b2_all, g_all, b_all)


def _proj_kernel(x_ref, w_ref, o_ref, xb_ref):
    @pl.when(pl.program_id(1) == 0)
    def _():
        xb_ref[...] = x_ref[...].astype(BF16)

    o_ref[...] = jnp.dot(xb_ref[...], w_ref[...], preferred_element_type=F32)


def _gla_proj(x2d, layer, w_in_all):
    m = x2d.shape[0]
    tm = min(1024, m)
    return pl.pallas_call(
        _proj_kernel,
        grid=(m // tm, PROJ_PAD // PROJ_TILE),
        in_specs=[
            pl.BlockSpec((tm, D_MODEL), lambda i, n: (i, 0)),
            pl.BlockSpec((None, D_MODEL, PROJ_TILE), lambda i, n: (layer, 0, n)),
        ],
        out_specs=pl.BlockSpec((tm, PROJ_TILE), lambda i, n: (i, n)),
        out_shape=jax.ShapeDtypeStruct((m, PROJ_PAD), F32),
        scratch_shapes=[pltpu.VMEM((tm, D_MODEL), BF16)],
        compiler_params=_params("parallel", "arbitrary"),
        name="gla_proj",
    )(x2d, w_in_all)


def _rms_gate(o, og, nw):
    o = o * lax.rsqrt(jnp.mean(o * o, axis=-1, keepdims=True) + RMS_EPS) * nw
    return o * (og * jax.nn.sigmoid(og))


def _gla_block(r0, q_ref, k_ref, v_ref, og_ref, nw_ref, o_ref, s_ref, bc_ref, a_ref):
    scale = GLA_DK ** -0.5
    half = GLA_CHUNK // 2
    chunks = range(CHUNKS_PER_BLOCK)
    blk = slice(r0, r0 + GLA_BLOCK)
    a_blk = a_ref.at[blk]

    tot = [bc_ref[r0 + (c + 1) * GLA_CHUNK - 1:r0 + (c + 1) * GLA_CHUNK, :] for c in chunks]
    before = [jnp.zeros_like(tot[0])]
    for c in chunks[1:]:
        before.append(before[-1] + tot[c - 1])
    after = [jnp.zeros_like(tot[0])]
    for c in reversed(chunks[:-1]):
        after.insert(0, after[0] + tot[c + 1])
    total = before[-1] + tot[-1]

    v = v_ref[blk, :].astype(BF16)
    qs, ks, bcs, qd, kd, q_hi, k_lo = [], [], [], [], [], [], []
    for c in chunks:
        rows = slice(r0 + c * GLA_CHUNK, r0 + (c + 1) * GLA_CHUNK)
        bcc = bc_ref[rows, :]
        qc = q_ref[rows, :] * scale
        kc = k_ref[rows, :]
        mid = bcc[half:half + 1]
        qs.append(qc)
        ks.append(kc)
        bcs.append(bcc)
        qd.append(qc * jnp.exp2(bcc))
        kd.append(kc * jnp.exp2(tot[c] - bcc))
        q_hi.append(qc[half:] * jnp.exp2(bcc[half:] - mid))
        k_lo.append(kc[:half] * jnp.exp2(mid - bcc[:half]))

    s0 = s_ref[...]
    q_in = jnp.concatenate([qd[c] * jnp.exp2(before[c]) for c in chunks], axis=0)
    o = jnp.dot(q_in.astype(BF16), s0.astype(BF16), preferred_element_type=F32)

    kd_all = jnp.concatenate(kd, axis=0).astype(BF16)
    lhs = []
    for jc in chunks[:-1]:
        lhs.append(qd[jc + 1])
        lhs.extend(qd[c] * jnp.exp2(before[c] - before[jc + 1]) for c in chunks[jc + 2:])
    pair = _dot_nt(jnp.concatenate(lhs, axis=0).astype(BF16), kd_all)
    a_blk[...] = jnp.zeros((GLA_BLOCK, GLA_BLOCK), F32)
    off = 0
    for jc in chunks[:-1]:
        lo = (jc + 1) * GLA_CHUNK
        n = GLA_BLOCK - lo
        lane = lax.broadcasted_iota(jnp.int32, (n, GLA_BLOCK), 1)
        keep = (lane >= jc * GLA_CHUNK) & (lane < lo)
        a_blk[lo:, :] += jnp.where(keep, pair[off:off + n], 0.0)
        off += n

    zeros = jnp.zeros((half, GLA_DK), F32)
    k_lo_all = jnp.concatenate([t for c in chunks for t in (k_lo[c], zeros)], axis=0).astype(BF16)
    cross = _dot_nt(jnp.concatenate(q_hi, axis=0).astype(BF16), k_lo_all)
    row = lax.broadcasted_iota(jnp.int32, (half, GLA_DK), 0)
    lane = lax.broadcasted_iota(jnp.int32, (half, GLA_BLOCK), 1)
    for c in chunks:
        for hs in range(2):
            rs = slice(hs * half, (hs + 1) * half)
            qh, kh, bh = qs[c][rs], ks[c][rs], bcs[c][rs]
            col0 = c * GLA_CHUNK + hs * half
            acc = jnp.zeros((half, GLA_BLOCK), F32)
            if hs == 1:
                in_chunk = (lane >= c * GLA_CHUNK) & (lane < col0)
                acc = jnp.where(in_chunk, cross[c * half:(c + 1) * half], 0.0)
            for j in range(half):
                d = bh - bh[j:j + 1]
                e = jnp.exp2(jnp.where(row >= j, d, -jnp.inf)) if j else jnp.exp2(d)
                t = qh * e * kh[j:j + 1]
                acc = acc + jnp.where(lane == col0 + j, jnp.sum(t, axis=1, keepdims=True), 0.0)
            a_blk[col0:col0 + half, :] += acc

    o = o + jnp.dot(a_blk[...].astype(BF16), v, preferred_element_type=F32)
    o_ref[blk, :] = _rms_gate(o, og_ref[blk, :], nw_ref[...]).astype(o_ref.dtype)

    k_out = jnp.concatenate([kd[c] * jnp.exp2(after[c]) for c in chunks], axis=0)
    upd = _dot_tn(k_out.astype(BF16), v)
    gcol = jnp.transpose(jnp.broadcast_to(jnp.exp2(total), (LANES, GLA_DK)))
    for n in range(GLA_DV // LANES):
        cols = slice(n * LANES, (n + 1) * LANES)
        s_ref[:, cols] = s0[:, cols] * gcol + upd[:, cols]


def _gla_rec_kernel(q_ref, k_ref, v_ref, og_ref, gl_ref, wgu_ref, gb_ref, nw_ref, tril_ref,
                    o_ref, sout_ref, s_ref, bc_ref, a_ref):
    l = pl.program_id(2)

    @pl.when(l == 0)
    def _():
        s_ref[...] = jnp.zeros_like(s_ref)

    gk = jnp.dot(gl_ref[...].astype(BF16), wgu_ref[...], preferred_element_type=F32) + gb_ref[...]
    la = _log_sigmoid(gk) * (LOG2_E / GLA_GATE_TEMP)
    blocks = q_ref.shape[0] // GLA_BLOCK
    for i in range(blocks):
        blk = slice(i * GLA_BLOCK, (i + 1) * GLA_BLOCK)
        bc_ref[blk, :] = _mask_dot(tril_ref[...], la[blk], 2)
    for i in range(blocks):
        _gla_block(i * GLA_BLOCK, q_ref, k_ref, v_ref, og_ref, nw_ref, o_ref, s_ref, bc_ref, a_ref)

    @pl.when(l == pl.num_programs(2) - 1)
    def _():
        sout_ref[...] = s_ref[...]


def _chunk_tril():
    i = jnp.arange(GLA_BLOCK)[:, None]
    j = jnp.arange(GLA_BLOCK)[None, :]
    return ((i // GLA_CHUNK == j // GLA_CHUNK) & (j <= i)).astype(BF16)


def _gla_rec_prompt(proj, batch, seq, layer, wgu_all, gb_all, nw_all):
    tl = min(512, seq)
    nl = seq // tl
    kq = GLA_KEY_DIM // GLA_DK
    kv = 2 * GLA_KEY_DIM // GLA_DV
    row = lambda b, h, l: b * nl + l
    o, s_final = pl.pallas_call(
        _gla_rec_kernel,
        grid=(batch, GLA_HEADS, nl),
        in_specs=[
            pl.BlockSpec((tl, GLA_DK), lambda b, h, l: (row(b, h, l), h)),
            pl.BlockSpec((tl, GLA_DK), lambda b, h, l: (row(b, h, l), kq + h)),
            pl.BlockSpec((tl, GLA_DV), lambda b, h, l: (row(b, h, l), kv + h)),
            pl.BlockSpec((tl, GLA_DV), lambda b, h, l: (row(b, h, l), kv + GLA_HEADS + h)),
            pl.BlockSpec((tl, LANES), lambda b, h, l: (row(b, h, l), PROJ_QKVG // LANES)),
            pl.BlockSpec((None, LANES, GLA_DK), lambda b, h, l: (layer, 0, h)),
            pl.BlockSpec((None, 1, GLA_DK), lambda b, h, l: (layer, 0, h)),
            _layer_spec(layer, 1, GLA_DV),
            pl.BlockSpec((GLA_BLOCK, GLA_BLOCK), lambda b, h, l: (0, 0)),
        ],
        out_specs=[
            pl.BlockSpec((tl, GLA_DV), lambda b, h, l: (row(b, h, l), h)),
            pl.BlockSpec((None, None, GLA_DK, GLA_DV), lambda b, h, l: (b, h, 0, 0)),
        ],
        out_shape=[
            jax.ShapeDtypeStruct((batch * seq, GLA_VAL_DIM), BF16),
            jax.ShapeDtypeStruct((batch, GLA_HEADS, GLA_DK, GLA_DV), F32),
        ],
        scratch_shapes=[
            pltpu.VMEM((GLA_DK, GLA_DV), F32),
            pltpu.VMEM((tl, GLA_DK), F32),
            pltpu.VMEM((tl, GLA_BLOCK), F32),
        ],
        compiler_params=_params("parallel", "parallel", "arbitrary"),
        name="gla_rec_prompt",
    )(proj, proj, proj, proj, proj, wgu_all, gb_all, nw_all, _chunk_tril())
    return o, s_final


def _gla_sample_kernel(q_ref, k_ref, v_ref, og_ref, gl_ref, s_ref, wgu_ref, gb_ref, nw_ref, *rest, tb):
    o_ref, sout_ref = rest[-2:]
    gk = jnp.dot(gl_ref[...].astype(BF16), wgu_ref[...], preferred_element_type=F32) + gb_ref[...]
    a = jnp.exp(_log_sigmoid(gk) / GLA_GATE_TEMP)
    q = q_ref[...] * (GLA_DK ** -0.5)
    k = k_ref[...]
    v = v_ref[...]
    qk = jnp.sum(q * k, axis=1, keepdims=True)
    pad = jnp.zeros((LANES - 3 * tb, GLA_DK), F32)
    cols = jnp.transpose(jnp.concatenate([a, k, q * a, pad], axis=0))
    outs = []
    for b in range(tb):
        s0 = s_ref[b]
        vb = v[b:b + 1]
        outs.append(qk[b:b + 1] * vb + jnp.sum(cols[:, 2 * tb + b:2 * tb + b + 1] * s0, axis=0, keepdims=True))
        sout_ref[b] = s0 * cols[:, b:b + 1] + cols[:, tb + b:tb + b + 1] * vb
    o_ref[...] = _rms_gate(jnp.concatenate(outs, axis=0), og_ref[...], nw_ref[...])


def _gla_rec_sample(proj, state_gla, new_state, layer, wgu_all, gb_all, nw_all):
    bs = proj.shape[0]
    tb = SUBLANES
    kq = GLA_KEY_DIM // GLA_DK
    kv = 2 * GLA_KEY_DIM // GLA_DV
    s_spec = pl.BlockSpec((None, tb, None, GLA_DK, GLA_DV), lambda i, h: (layer, i, h, 0, 0))
    in_specs = [
        pl.BlockSpec((tb, GLA_DK), lambda i, h: (i, h)),
        pl.BlockSpec((tb, GLA_DK), lambda i, h: (i, kq + h)),
        pl.BlockSpec((tb, GLA_DV), lambda i, h: (i, kv + h)),
        pl.BlockSpec((tb, GLA_DV), lambda i, h: (i, kv + GLA_HEADS + h)),
        pl.BlockSpec((tb, LANES), lambda i, h: (i, PROJ_QKVG // LANES)),
        s_spec,
        pl.BlockSpec((None, LANES, GLA_DK), lambda i, h: (layer, 0, h)),
        pl.BlockSpec((None, 1, GLA_DK), lambda i, h: (layer, 0, h)),
        _layer_spec(layer, 1, GLA_DV),
    ]
    args = [proj, proj, proj, proj, proj, state_gla, wgu_all, gb_all, nw_all]
    aliases = {}
    if new_state is not None:
        in_specs.append(pl.BlockSpec(memory_space=pl.ANY))
        args.append(new_state)
        aliases = {len(args) - 1: 1}
    return pl.pallas_call(
        functools.partial(_gla_sample_kernel, tb=tb),
        grid=(bs // tb, GLA_HEADS),
        in_specs=in_specs,
        out_specs=[pl.BlockSpec((tb, GLA_DV), lambda i, h: (i, h)), s_spec],
        out_shape=[
            jax.ShapeDtypeStruct((bs, GLA_VAL_DIM), F32),
            jax.ShapeDtypeStruct(state_gla.shape, F32),
        ],
        input_output_aliases=aliases,
        compiler_params=_params("parallel", "parallel"),
        name="gla_rec_sample",
    )(*args)


def _out_ln_kernel(o_ref, w_ref, x_ref, g_ref, b_ref, y_ref):
    tm = x_ref.shape[0]
    rb = min(OUT_ROWS, tm)
    for r in range(tm // rb):
        rows = slice(r * rb, (r + 1) * rb)
        y = jnp.dot(o_ref[rows, :].astype(BF16), w_ref[...], preferred_element_type=F32)
        y_ref[rows, :] = _layer_norm(DN_ALPHA * x_ref[rows, :] + y, g_ref[...], b_ref[...])


def _out_ln(o2d, x2d, layer, mix_layer, w_out_all, g_all, b_all):
    m = x2d.shape[0]
    tm = min(512, m)
    return pl.pallas_call(
        _out_ln_kernel,
        grid=(m // tm,),
        in_specs=[
            pl.BlockSpec((tm, GLA_VAL_DIM), lambda i: (i, 0)),
            _layer_spec(layer, GLA_VAL_DIM, D_MODEL),
            pl.BlockSpec((tm, D_MODEL), lambda i: (i, 0)),
            _layer_spec(mix_layer, 1, D_MODEL), _layer_spec(mix_layer, 1, D_MODEL),
        ],
        out_specs=pl.BlockSpec((tm, D_MODEL), lambda i: (i, 0)),
        out_shape=jax.ShapeDtypeStruct(x2d.shape, F32),
        compiler_params=_params("parallel"),
        name="gla_out_ln",
    )(o2d, w_out_all, x2d, g_all, b_all)


def kernel(x_prompt, x_sample, state_pool, state_gla, pool_w, pool_scale, gla_w_in, gla_w_gate_up,
           gla_gate_bias, gla_norm_w, gla_w_out, ln_mix_g, ln_mix_b, mlp_w1, mlp_b1, mlp_w2, mlp_b2,
           ln_ffn_g, ln_ffn_b):
    batch, seq, _ = x_prompt.shape
    bs = x_sample.shape[0]
    xp = x_prompt.reshape(batch * seq, D_MODEL)
    xs = x_sample.reshape(bs, D_MODEL)

    vec = lambda a: a.reshape(a.shape[0], 1, a.shape[1])
    pool_w16, w1, w2, w_out = (a.astype(BF16) for a in (pool_w, mlp_w1, mlp_w2, gla_w_out))
    w_in = jnp.pad(gla_w_in.astype(BF16), ((0, 0), (0, 0), (0, PROJ_PAD - gla_w_in.shape[-1])))
    wgu = jnp.pad(gla_w_gate_up.astype(BF16), ((0, 0), (0, LANES - GLA_GATE_RANK), (0, 0)))
    pool_scale, gate_bias, norm_w, mix_g, mix_b, b1, b2, ffn_g, ffn_b = map(
        vec, (pool_scale, gla_gate_bias, gla_norm_w, ln_mix_g, ln_mix_b, mlp_b1, mlp_b2, ln_ffn_g, ln_ffn_b))

    state_rows = jnp.transpose(state_pool, (0, 2, 1, 3))
    pool_in_p, pool_in_s, gla_states_p = [], [], []
    new_gla_sample = None
    for i in range(DEPTH):
        j = i // 2
        if i % 2 == 0:
            pool_in_p.append(xp.reshape(batch, seq, D_MODEL)[:, seq - POOL_CTX:])
            pool_in_s.append(xs)
            xp = _pool_ln_prompt(xp, batch, seq, j, i, pool_w16, pool_scale, mix_g, mix_b)
            xs = _pool_ln_sample(xs, state_rows, j, i, pool_w16, pool_scale, mix_g, mix_b)
        else:
            o_p, s_p = _gla_rec_prompt(_gla_proj(xp, j, w_in), batch, seq, j, wgu, gate_bias, norm_w)
            gla_states_p.append(s_p)
            o_s, new_gla_sample = _gla_rec_sample(_gla_proj(xs, j, w_in), state_gla, new_gla_sample, j, wgu,
                                                  gate_bias, norm_w)
            xp = _out_ln(o_p, xp, j, i, w_out, mix_g, mix_b)
            xs = _out_ln(o_s, xs, j, i, w_out, mix_g, mix_b)
        xp = _mlp_ln(xp, i, w1, b1, w2, b2, ffn_g, ffn_b)
        xs = _mlp_ln(xs, i, w1, b1, w2, b2, ffn_g, ffn_b)

    new_rows = jnp.concatenate([state_rows[:, 1:], jnp.stack(pool_in_s)[:, None]], axis=1)
    new_pool_sample = jnp.transpose(new_rows, (0, 2, 1, 3))
    return (xp.reshape(batch, seq, D_MODEL), xs.reshape(bs, 1, D_MODEL), jnp.stack(pool_in_p),
            jnp.stack(gla_states_p), new_pool_sample, new_gla_sample)
```

```python
import functools

import jax
import jax.numpy as jnp
from jax import lax
from jax.experimental import pallas as pl
from jax.experimental.pallas import tpu as pltpu

F32 = jnp.float32
BF16 = jnp.bfloat16

D_MODEL = 2048
DEPTH = 4
PAST_LEN = 16384
POOL_WINDOWS = (2, 4, 8, 16)
POOL_GROUP = D_MODEL // len(POOL_WINDOWS)
POOL_CTX = max(POOL_WINDOWS) - 1
GLA_HEADS = 4
GLA_KEY_DIM = D_MODEL // 2
GLA_VAL_DIM = D_MODEL
GLA_DK = GLA_KEY_DIM // GLA_HEADS
GLA_DV = GLA_VAL_DIM // GLA_HEADS
GLA_GATE_RANK = 16
GLA_GATE_TEMP = 16.0
GLA_CHUNK = 16
D_FF = 4 * D_MODEL
LN_EPS = 1e-5
RMS_EPS = 1e-5
DN_ALPHA = (2 * DEPTH) ** 0.25
LOG2_E = 1.4426950408889634

LANES = 128
SUBLANES = 8
MXU_DIM = 256
VMEM_LIMIT_BYTES = 60 * 1024 * 1024

PROJ_QKVG = 2 * GLA_KEY_DIM + 2 * GLA_VAL_DIM
PROJ_TILE = 5 * MXU_DIM
PROJ_PAD = -(-(PROJ_QKVG + LANES) // PROJ_TILE) * PROJ_TILE
GLA_BLOCK = 128
CHUNKS_PER_BLOCK = GLA_BLOCK // GLA_CHUNK
POOL_ROWS = 256
MLP_ROWS = 256
OUT_ROWS = 128


def _params(*sem):
    return pltpu.CompilerParams(dimension_semantics=sem, vmem_limit_bytes=VMEM_LIMIT_BYTES)


def _layer_spec(layer, *block):
    return pl.BlockSpec((None,) + block, lambda *_: (layer,) + (0,) * len(block))


def _layer_norm(v, g, b):
    mu = jnp.mean(v, axis=-1, keepdims=True)
    c = v - mu
    var = jnp.mean(c * c, axis=-1, keepdims=True)
    return c * lax.rsqrt(var + LN_EPS) * g + b


def _split_bf16(z, parts):
    out = []
    for _ in range(parts - 1):
        hi = z.astype(BF16)
        out.append(hi)
        z = z - hi.astype(F32)
    out.append(z.astype(BF16))
    return out


def _mask_dot(mask_bf16, z, parts):
    return sum(jnp.dot(mask_bf16, t, preferred_element_type=F32) for t in _split_bf16(z, parts))


def _log_sigmoid(x):
    return jnp.minimum(x, 0.0) - jnp.log(1.0 + jnp.exp(-jnp.abs(x)))


def _dot_nt(a, b):
    return lax.dot_general(a, b, (((1,), (1,)), ((), ())), preferred_element_type=F32)


def _dot_tn(a, b):
    return lax.dot_general(a, b, (((0,), (0,)), ((), ())), preferred_element_type=F32)


def _pool_ln_kernel(x_ref, halo_ref, bm_ref, bh_ref, w_ref, scale_ref, g_ref, b_ref, o_ref, *, tile_rows):
    j = pl.program_id(1)
    rb = min(POOL_ROWS, tile_rows)
    row = lax.broadcasted_iota(jnp.int32, (rb, 1), 0)
    for g, w in enumerate(POOL_WINDOWS):
        cols = slice(g * POOL_GROUP, (g + 1) * POOL_GROUP)
        for r in range(tile_rows // rb):
            xb = x_ref[r * rb:(r + 1) * rb, cols]
            if r == 0:
                halo = jnp.where(j == 0, 0.0, halo_ref[:, cols])
            else:
                halo = x_ref[r * rb - 16:r * rb, cols]
            win = _mask_dot(bm_ref[g], xb, 2)
            head = win[:16] + _mask_dot(bh_ref[g], halo, 2)
            win = jnp.concatenate([head, win[16:]], axis=0)
            count = jnp.minimum(j * tile_rows + r * rb + row + 1, w).astype(F32)
            p = win / count - xb
            y = jnp.dot(p.astype(BF16), w_ref[g], preferred_element_type=F32)
            o_ref[r * rb:(r + 1) * rb, cols] = y * scale_ref[:, cols]
    ln_rows = min(128, tile_rows)
    for r in range(tile_rows // ln_rows):
        rows = slice(r * ln_rows, (r + 1) * ln_rows)
        v = DN_ALPHA * x_ref[rows, :] + o_ref[rows, :]
        o_ref[rows, :] = _layer_norm(v, g_ref[...], b_ref[...])


def _band_matrices(rb):
    i = jnp.arange(rb)[:, None]
    jj = jnp.arange(rb)[None, :]
    bm = jnp.stack([((i - jj >= 0) & (i - jj < w)) for w in POOL_WINDOWS]).astype(BF16)
    i16 = jnp.arange(16)[:, None]
    j16 = jnp.arange(16)[None, :]
    bh = jnp.stack([(j16 >= i16 + 17 - w) for w in POOL_WINDOWS]).astype(BF16)
    return bm, bh


def _pool_ln_prompt(x2d, batch, seq, layer, mix_layer, w_all, scale_all, g_all, b_all):
    tile_rows = min(1024, seq)
    nt = seq // tile_rows
    bm, bh = _band_matrices(min(POOL_ROWS, tile_rows))
    const = lambda *shape: pl.BlockSpec(shape, lambda bi, j: (0,) * len(shape))
    return pl.pallas_call(
        functools.partial(_pool_ln_kernel, tile_rows=tile_rows),
        grid=(batch, nt),
        in_specs=[
            pl.BlockSpec((tile_rows, D_MODEL), lambda bi, j: (bi * nt + j, 0)),
            pl.BlockSpec((16, D_MODEL), lambda bi, j: (jnp.maximum((bi * nt + j) * (tile_rows // 16) - 1, 0), 0)),
            const(*bm.shape), const(*bh.shape),
            _layer_spec(layer, *w_all.shape[1:]),
            _layer_spec(layer, 1, D_MODEL), _layer_spec(mix_layer, 1, D_MODEL), _layer_spec(mix_layer, 1, D_MODEL),
        ],
        out_specs=pl.BlockSpec((tile_rows, D_MODEL), lambda bi, j: (bi * nt + j, 0)),
        out_shape=jax.ShapeDtypeStruct(x2d.shape, F32),
        compiler_params=_params("parallel", "arbitrary"),
        name="pool_ln_prompt",
    )(x2d, x2d, bm, bh, w_all, scale_all, g_all, b_all)


def _pool_ln_sample_kernel(x_ref, st_ref, w_ref, scale_ref, g_ref, b_ref, o_ref):
    x = x_ref[...]
    for g, w in enumerate(POOL_WINDOWS):
        cols = slice(g * POOL_GROUP, (g + 1) * POOL_GROUP)
        xg = x[:, cols]
        win = xg
        for r in range(POOL_CTX - (w - 1), POOL_CTX):
            win = win + st_ref[r, :, cols]
        p = win / float(min(PAST_LEN + 1, w)) - xg
        o_ref[:, cols] = jnp.dot(p.astype(BF16), w_ref[g], preferred_element_type=F32) * scale_ref[:, cols]
    o_ref[...] = _layer_norm(DN_ALPHA * x + o_ref[...], g_ref[...], b_ref[...])


def _pool_ln_sample(xs, state_rows, layer, mix_layer, w_all, scale_all, g_all, b_all):
    bs = xs.shape[0]
    tb = min(32, bs)
    return pl.pallas_call(
        _pool_ln_sample_kernel,
        grid=(bs // tb,),
        in_specs=[
            pl.BlockSpec((tb, D_MODEL), lambda i: (i, 0)),
            pl.BlockSpec((None, POOL_CTX, tb, D_MODEL), lambda i: (layer, 0, i, 0)),
            _layer_spec(layer, *w_all.shape[1:]),
            _layer_spec(layer, 1, D_MODEL), _layer_spec(mix_layer, 1, D_MODEL), _layer_spec(mix_layer, 1, D_MODEL),
        ],
        out_specs=pl.BlockSpec((tb, D_MODEL), lambda i: (i, 0)),
        out_shape=jax.ShapeDtypeStruct(xs.shape, F32),
        compiler_params=_params("parallel"),
        name="pool_ln_sample",
    )(xs, state_rows, w_all, scale_all, g_all, b_all)


def _mlp_ln_kernel(x_ref, w1_ref, b1_ref, w2_ref, b2_ref, g_ref, b_ref, o_ref, xb_ref):
    f = pl.program_id(1)

    @pl.when(f == 0)
    def _():
        xb_ref[...] = x_ref[...].astype(BF16)
        o_ref[...] = jnp.zeros_like(o_ref)

    tm = x_ref.shape[0]
    rb = min(MLP_ROWS, tm)

    def accumulate(finish):
        for r in range(tm // rb):
            rows = slice(r * rb, (r + 1) * rb)
            h = jnp.dot(xb_ref[rows, :], w1_ref[...], preferred_element_type=F32) + b1_ref[...]
            h = jnp.square(jnp.maximum(h, 0.0))
            acc = o_ref[rows, :] + jnp.dot(h.astype(BF16), w2_ref[...], preferred_element_type=F32)
            if finish:
                v = DN_ALPHA * x_ref[rows, :] + (acc + b2_ref[...])
                acc = _layer_norm(v, g_ref[...], b_ref[...])
            o_ref[rows, :] = acc

    last = pl.num_programs(1) - 1
    pl.when(f < last)(functools.partial(accumulate, False))
    pl.when(f == last)(functools.partial(accumulate, True))


def _mlp_ln(x2d, layer, w1, b1_all, w2, b2_all, g_all, b_all):
    m = x2d.shape[0]
    tm = min(1024, m)
    tf = 1024
    vec = lambda: pl.BlockSpec((None, 1, D_MODEL), lambda i, f: (layer, 0, 0))
    return pl.pallas_call(
        _mlp_ln_kernel,
        grid=(m // tm, D_FF // tf),
        in_specs=[
            pl.BlockSpec((tm, D_MODEL), lambda i, f: (i, 0)),
            pl.BlockSpec((D_MODEL, tf), lambda i, f: (0, f)),
            pl.BlockSpec((None, 1, tf), lambda i, f: (layer, 0, f)),
            pl.BlockSpec((tf, D_MODEL), lambda i, f: (f, 0)),
            vec(), vec(), vec(),
        ],
        out_specs=pl.BlockSpec((tm, D_MODEL), lambda i, f: (i, 0)),
        out_shape=jax.ShapeDtypeStruct(x2d.shape, F32),
        scratch_shapes=[pltpu.VMEM((tm, D_MODEL), BF16)],
        compiler_params=_params("parallel", "arbitrary"),
        name="mlp_ln",
    )(x2d, w1, b1_all, w2, b2_all, g_all, b_all)


class _CastJob:
    def __init__(self, w1_all, w2_all, layer, steps, step_of):
        nb = 1 << (steps.bit_length() - 1)
        slab = lambda *idx: jnp.minimum(step_of(*idx), nb - 1)
        r1, r2 = D_MODEL // nb, D_FF // nb
        self.args = [w1_all, w2_all]
        self.in_specs = [pl.BlockSpec((None, r1, D_FF), lambda *idx: (layer, slab(*idx), 0)),
                         pl.BlockSpec((None, r2, D_MODEL), lambda *idx: (layer, slab(*idx), 0))]
        self.out_specs = [pl.BlockSpec((r1, D_FF), lambda *idx: (slab(*idx), 0)),
                          pl.BlockSpec((r2, D_MODEL), lambda *idx: (slab(*idx), 0))]
        self.out_shape = [jax.ShapeDtypeStruct((D_MODEL, D_FF), BF16),
                          jax.ShapeDtypeStruct((D_FF, D_MODEL), BF16)]


def _cast_slabs(src1_ref, src2_ref, dst1_ref, dst2_ref):
    dst1_ref[...] = src1_ref[...].astype(BF16)
    dst2_ref[...] = src2_ref[...].astype(BF16)


def _proj_kernel(x_ref, w_ref, *rest, cast):
    if cast:
        src1_ref, src2_ref, o_ref, dst1_ref, dst2_ref, xb_ref = rest
        _cast_slabs(src1_ref, src2_ref, dst1_ref, dst2_ref)
    else:
        o_ref, xb_ref = rest

    @pl.when(pl.program_id(1) == 0)
    def _():
        xb_ref[...] = x_ref[...].astype(BF16)

    o_ref[...] = jnp.dot(xb_ref[...], w_ref[...], preferred_element_type=F32)


def _gla_proj(x2d, layer, w_in_all, cast_weights=None):
    m = x2d.shape[0]
    tm = min(1024, m)
    nn = PROJ_PAD // PROJ_TILE
    job = None
    if cast_weights is not None:
        job = _CastJob(*cast_weights, steps=(m // tm) * nn, step_of=lambda i, n: i * nn + n)
    outs = pl.pallas_call(
        functools.partial(_proj_kernel, cast=job is not None),
        grid=(m // tm, nn),
        in_specs=[
            pl.BlockSpec((tm, D_MODEL), lambda i, n: (i, 0)),
            pl.BlockSpec((None, D_MODEL, PROJ_TILE), lambda i, n: (layer, 0, n)),
        ] + (job.in_specs if job else []),
        out_specs=[pl.BlockSpec((tm, PROJ_TILE), lambda i, n: (i, n))] + (job.out_specs if job else []),
        out_shape=[jax.ShapeDtypeStruct((m, PROJ_PAD), F32)] + (job.out_shape if job else []),
        scratch_shapes=[pltpu.VMEM((tm, D_MODEL), BF16)],
        compiler_params=_params("arbitrary", "arbitrary"),
        name="gla_proj",
    )(x2d, w_in_all, *(job.args if job else []))
    return outs[0], tuple(outs[1:])


def _rms_gate(o, og, nw):
    o = o * lax.rsqrt(jnp.mean(o * o, axis=-1, keepdims=True) + RMS_EPS) * nw
    return o * (og * jax.nn.sigmoid(og))


def _gla_block(r0, q_ref, k_ref, v_ref, og_ref, nw_ref, o_ref, s_ref, bc_ref, a_ref):
    scale = GLA_DK ** -0.5
    half = GLA_CHUNK // 2
    chunks = range(CHUNKS_PER_BLOCK)
    blk = slice(r0, r0 + GLA_BLOCK)
    a_blk = a_ref.at[blk]

    tot = [bc_ref[r0 + (c + 1) * GLA_CHUNK - 1:r0 + (c + 1) * GLA_CHUNK, :] for c in chunks]
    before = [jnp.zeros_like(tot[0])]
    for c in chunks[1:]:
        before.append(before[-1] + tot[c - 1])
    after = [jnp.zeros_like(tot[0])]
    for c in reversed(chunks[:-1]):
        after.insert(0, after[0] + tot[c + 1])
    total = before[-1] + tot[-1]

    v = v_ref[blk, :].astype(BF16)
    qs, ks, bcs, qd, kd, q_hi, k_lo = [], [], [], [], [], [], []
    for c in chunks:
        rows = slice(r0 + c * GLA_CHUNK, r0 + (c + 1) * GLA_CHUNK)
        bcc = bc_ref[rows, :]
        qc = q_ref[rows, :] * scale
        kc = k_ref[rows, :]
        mid = bcc[half:half + 1]
        qs.append(qc)
        ks.append(kc)
        bcs.append(bcc)
        qd.append(qc * jnp.exp2(bcc))
        kd.append(kc * jnp.exp2(tot[c] - bcc))
        q_hi.append(qc[half:] * jnp.exp2(bcc[half:] - mid))
        k_lo.append(kc[:half] * jnp.exp2(mid - bcc[:half]))

    s0 = s_ref[...]
    q_in = jnp.concatenate([qd[c] * jnp.exp2(before[c]) for c in chunks], axis=0)
    o = jnp.dot(q_in.astype(BF16), s0.astype(BF16), preferred_element_type=F32)

    kd_all = jnp.concatenate(kd, axis=0).astype(BF16)
    lhs = []
    for jc in chunks[:-1]:
        lhs.append(qd[jc + 1])
        lhs.extend(qd[c] * jnp.exp2(before[c] - before[jc + 1]) for c in chunks[jc + 2:])
    pair = _dot_nt(jnp.concatenate(lhs, axis=0).astype(BF16), kd_all)
    a_blk[...] = jnp.zeros((GLA_BLOCK, GLA_BLOCK), F32)
    off = 0
    for jc in chunks[:-1]:
        lo = (jc + 1) * GLA_CHUNK
        n = GLA_BLOCK - lo
        lane = lax.broadcasted_iota(jnp.int32, (n, GLA_BLOCK), 1)
        keep = (lane >= jc * GLA_CHUNK) & (lane < lo)
        a_blk[lo:, :] += jnp.where(keep, pair[off:off + n], 0.0)
        off += n

    zeros = jnp.zeros((half, GLA_DK), F32)
    k_lo_all = jnp.concatenate([t for c in chunks for t in (k_lo[c], zeros)], axis=0).astype(BF16)
    cross = _dot_nt(jnp.concatenate(q_hi, axis=0).astype(BF16), k_lo_all)
    row = lax.broadcasted_iota(jnp.int32, (half, GLA_DK), 0)
    lane = lax.broadcasted_iota(jnp.int32, (half, GLA_BLOCK), 1)
    for c in chunks:
        for hs in range(2):
            rs = slice(hs * half, (hs + 1) * half)
            qh, kh, bh = qs[c][rs], ks[c][rs], bcs[c][rs]
            col0 = c * GLA_CHUNK + hs * half
            acc = jnp.zeros((half, GLA_BLOCK), F32)
            if hs == 1:
                in_chunk = (lane >= c * GLA_CHUNK) & (lane < col0)
                acc = jnp.where(in_chunk, cross[c * half:(c + 1) * half], 0.0)
            for j in range(half):
                d = bh - bh[j:j + 1]
                e = jnp.exp2(jnp.where(row >= j, d, -jnp.inf)) if j else jnp.exp2(d)
                t = qh * e * kh[j:j + 1]
                acc = acc + jnp.where(lane == col0 + j, jnp.sum(t, axis=1, keepdims=True), 0.0)
            a_blk[col0:col0 + half, :] += acc

    o = o + jnp.dot(a_blk[...].astype(BF16), v, preferred_element_type=F32)
    o_ref[blk, :] = _rms_gate(o, og_ref[blk, :], nw_ref[...]).astype(o_ref.dtype)

    k_out = jnp.concatenate([kd[c] * jnp.exp2(after[c]) for c in chunks], axis=0)
    upd = _dot_tn(k_out.astype(BF16), v)
    gcol = jnp.transpose(jnp.broadcast_to(jnp.exp2(total), (LANES, GLA_DK)))
    for n in range(GLA_DV // LANES):
        cols = slice(n * LANES, (n + 1) * LANES)
        s_ref[:, cols] = s0[:, cols] * gcol + upd[:, cols]


def _gla_rec_kernel(q_ref, k_ref, v_ref, og_ref, gl_ref, wgu_ref, gb_ref, nw_ref, tril_ref, *rest, cast):
    if cast:
        src1_ref, src2_ref, o_ref, sout_ref, dst1_ref, dst2_ref, s_ref, bc_ref, a_ref = rest
        _cast_slabs(src1_ref, src2_ref, dst1_ref, dst2_ref)
    else:
        o_ref, sout_ref, s_ref, bc_ref, a_ref = rest
    l = pl.program_id(2)

    @pl.when(l == 0)
    def _():
        s_ref[...] = jnp.zeros_like(s_ref)

    gk = jnp.dot(gl_ref[...].astype(BF16), wgu_ref[...], preferred_element_type=F32) + gb_ref[...]
    la = _log_sigmoid(gk) * (LOG2_E / GLA_GATE_TEMP)
    blocks = q_ref.shape[0] // GLA_BLOCK
    for i in range(blocks):
        blk = slice(i * GLA_BLOCK, (i + 1) * GLA_BLOCK)
        bc_ref[blk, :] = _mask_dot(tril_ref[...], la[blk], 2)
    for i in range(blocks):
        _gla_block(i * GLA_BLOCK, q_ref, k_ref, v_ref, og_ref, nw_ref, o_ref, s_ref, bc_ref, a_ref)

    @pl.when(l == pl.num_programs(2) - 1)
    def _():
        sout_ref[...] = s_ref[...]


def _chunk_tril():
    i = jnp.arange(GLA_BLOCK)[:, None]
    j = jnp.arange(GLA_BLOCK)[None, :]
    return ((i // GLA_CHUNK == j // GLA_CHUNK) & (j <= i)).astype(BF16)


def _gla_rec_prompt(proj, batch, seq, layer, wgu_all, gb_all, nw_all, cast_weights=None):
    tl = min(512, seq)
    nl = seq // tl
    kq = GLA_KEY_DIM // GLA_DK
    kv = 2 * GLA_KEY_DIM // GLA_DV
    row = lambda b, h, l: b * nl + l
    job = None
    if cast_weights is not None:
        job = _CastJob(*cast_weights, steps=batch * GLA_HEADS * nl,
                       step_of=lambda b, h, l: (b * GLA_HEADS + h) * nl + l)
    outs = pl.pallas_call(
        functools.partial(_gla_rec_kernel, cast=job is not None),
        grid=(batch, GLA_HEADS, nl),
        in_specs=[
            pl.BlockSpec((tl, GLA_DK), lambda b, h, l: (row(b, h, l), h)),
            pl.BlockSpec((tl, GLA_DK), lambda b, h, l: (row(b, h, l), kq + h)),
            pl.BlockSpec((tl, GLA_DV), lambda b, h, l: (row(b, h, l), kv + h)),
            pl.BlockSpec((tl, GLA_DV), lambda b, h, l: (row(b, h, l), kv + GLA_HEADS + h)),
            pl.BlockSpec((tl, LANES), lambda b, h, l: (row(b, h, l), PROJ_QKVG // LANES)),
            pl.BlockSpec((None, LANES, GLA_DK), lambda b, h, l: (layer, 0, h)),
            pl.BlockSpec((None, 1, GLA_DK), lambda b, h, l: (layer, 0, h)),
            _layer_spec(layer, 1, GLA_DV),
            pl.BlockSpec((GLA_BLOCK, GLA_BLOCK), lambda b, h, l: (0, 0)),
        ] + (job.in_specs if job else []),
        out_specs=[
            pl.BlockSpec((tl, GLA_DV), lambda b, h, l: (row(b, h, l), h)),
            pl.BlockSpec((None, None, GLA_DK, GLA_DV), lambda b, h, l: (b, h, 0, 0)),
        ] + (job.out_specs if job else []),
        out_shape=[
            jax.ShapeDtypeStruct((batch * seq, GLA_VAL_DIM), BF16),
            jax.ShapeDtypeStruct((batch, GLA_HEADS, GLA_DK, GLA_DV), F32),
        ] + (job.out_shape if job else []),
        scratch_shapes=[
            pltpu.VMEM((GLA_DK, GLA_DV), F32),
            pltpu.VMEM((tl, GLA_DK), F32),
            pltpu.VMEM((tl, GLA_BLOCK), F32),
        ],
        compiler_params=_params("arbitrary", "arbitrary", "arbitrary"),
        name="gla_rec_prompt",
    )(proj, proj, proj, proj, proj, wgu_all, gb_all, nw_all, _chunk_tril(), *(job.args if job else []))
    return outs[0], outs[1], tuple(outs[2:])


def _gla_sample_kernel(q_ref, k_ref, v_ref, og_ref, gl_ref, s_ref, wgu_ref, gb_ref, nw_ref, *rest, tb):
    o_ref, sout_ref = rest[-2:]
    gk = jnp.dot(gl_ref[...].astype(BF16), wgu_ref[...], preferred_element_type=F32) + gb_ref[...]
    a = jnp.exp(_log_sigmoid(gk) / GLA_GATE_TEMP)
    q = q_ref[...] * (GLA_DK ** -0.5)
    k = k_ref[...]
    v = v_ref[...]
    qk = jnp.sum(q * k, axis=1, keepdims=True)
    pad = jnp.zeros((LANES - 3 * tb, GLA_DK), F32)
    cols = jnp.transpose(jnp.concatenate([a, k, q * a, pad], axis=0))
    outs = []
    for b in range(tb):
        s0 = s_ref[b]
        vb = v[b:b + 1]
        outs.append(qk[b:b + 1] * vb + jnp.sum(cols[:, 2 * tb + b:2 * tb + b + 1] * s0, axis=0, keepdims=True))
        sout_ref[b] = s0 * cols[:, b:b + 1] + cols[:, tb + b:tb + b + 1] * vb
    o_ref[...] = _rms_gate(jnp.concatenate(outs, axis=0), og_ref[...], nw_ref[...])


def _gla_rec_sample(proj, state_gla, new_state, layer, wgu_all, gb_all, nw_all):
    bs = proj.shape[0]
    tb = SUBLANES
    kq = GLA_KEY_DIM // GLA_DK
    kv = 2 * GLA_KEY_DIM // GLA_DV
    s_spec = pl.BlockSpec((None, tb, None, GLA_DK, GLA_DV), lambda i, h: (layer, i, h, 0, 0))
    in_specs = [
        pl.BlockSpec((tb, GLA_DK), lambda i, h: (i, h)),
        pl.BlockSpec((tb, GLA_DK), lambda i, h: (i, kq + h)),
        pl.BlockSpec((tb, GLA_DV), lambda i, h: (i, kv + h)),
        pl.BlockSpec((tb, GLA_DV), lambda i, h: (i, kv + GLA_HEADS + h)),
        pl.BlockSpec((tb, LANES), lambda i, h: (i, PROJ_QKVG // LANES)),
        s_spec,
        pl.BlockSpec((None, LANES, GLA_DK), lambda i, h: (layer, 0, h)),
        pl.BlockSpec((None, 1, GLA_DK), lambda i, h: (layer, 0, h)),
        _layer_spec(layer, 1, GLA_DV),
    ]
    args = [proj, proj, proj, proj, proj, state_gla, wgu_all, gb_all, nw_all]
    aliases = {}
    if new_state is not None:
        in_specs.append(pl.BlockSpec(memory_space=pl.ANY))
        args.append(new_state)
        aliases = {len(args) - 1: 1}
    return pl.pallas_call(
        functools.partial(_gla_sample_kernel, tb=tb),
        grid=(bs // tb, GLA_HEADS),
        in_specs=in_specs,
        out_specs=[pl.BlockSpec((tb, GLA_DV), lambda i, h: (i, h)), s_spec],
        out_shape=[
            jax.ShapeDtypeStruct((bs, GLA_VAL_DIM), F32),
            jax.ShapeDtypeStruct(state_gla.shape, F32),
        ],
        input_output_aliases=aliases,
        compiler_params=_params("parallel", "parallel"),
        name="gla_rec_sample",
    )(*args)


def _out_ln_kernel(o_ref, w_ref, x_ref, g_ref, b_ref, y_ref):
    tm = x_ref.shape[0]
    rb = min(OUT_ROWS, tm)
    for r in range(tm // rb):
        rows = slice(r * rb, (r + 1) * rb)
        y = jnp.dot(o_ref[rows, :].astype(BF16), w_ref[...], preferred_element_type=F32)
        y_ref[rows, :] = _layer_norm(DN_ALPHA * x_ref[rows, :] + y, g_ref[...], b_ref[...])


def _out_ln(o2d, x2d, layer, mix_layer, w_out_all, g_all, b_all):
    m = x2d.shape[0]
    tm = min(512, m)
    return pl.pallas_call(
        _out_ln_kernel,
        grid=(m // tm,),
        in_specs=[
            pl.BlockSpec((tm, GLA_VAL_DIM), lambda i: (i, 0)),
            _layer_spec(layer, GLA_VAL_DIM, D_MODEL),
            pl.BlockSpec((tm, D_MODEL), lambda i: (i, 0)),
            _layer_spec(mix_layer, 1, D_MODEL), _layer_spec(mix_layer, 1, D_MODEL),
        ],
        out_specs=pl.BlockSpec((tm, D_MODEL), lambda i: (i, 0)),
        out_shape=jax.ShapeDtypeStruct(x2d.shape, F32),
        compiler_params=_params("parallel"),
        name="gla_out_ln",
    )(o2d, w_out_all, x2d, g_all, b_all)


def kernel(x_prompt, x_sample, state_pool, state_gla, pool_w, pool_scale, gla_w_in, gla_w_gate_up,
           gla_gate_bias, gla_norm_w, gla_w_out, ln_mix_g, ln_mix_b, mlp_w1, mlp_b1, mlp_w2, mlp_b2,
           ln_ffn_g, ln_ffn_b):
    batch, seq, _ = x_prompt.shape
    bs = x_sample.shape[0]
    xp = x_prompt.reshape(batch * seq, D_MODEL)
    xs = x_sample.reshape(bs, D_MODEL)

    vec = lambda a: a.reshape(a.shape[0], 1, a.shape[1])
    pool_w16, w_out = pool_w.astype(BF16), gla_w_out.astype(BF16)
    mlp_w = {0: (mlp_w1[0].astype(BF16), mlp_w2[0].astype(BF16))}
    w_in = jnp.pad(gla_w_in.astype(BF16), ((0, 0), (0, 0), (0, PROJ_PAD - gla_w_in.shape[-1])))
    wgu = jnp.pad(gla_w_gate_up.astype(BF16), ((0, 0), (0, LANES - GLA_GATE_RANK), (0, 0)))
    pool_scale, gate_bias, norm_w, mix_g, mix_b, b1, b2, ffn_g, ffn_b = map(
        vec, (pool_scale, gla_gate_bias, gla_norm_w, ln_mix_g, ln_mix_b, mlp_b1, mlp_b2, ln_ffn_g, ln_ffn_b))

    state_rows = jnp.transpose(state_pool, (0, 2, 1, 3))
    pool_in_p, pool_in_s, gla_states_p = [], [], []
    new_gla_sample = None
    for i in range(DEPTH):
        j = i // 2
        if i % 2 == 0:
            pool_in_p.append(xp.reshape(batch, seq, D_MODEL)[:, seq - POOL_CTX:])
            pool_in_s.append(xs)
            xp = _pool_ln_prompt(xp, batch, seq, j, i, pool_w16, pool_scale, mix_g, mix_b)
            xs = _pool_ln_sample(xs, state_rows, j, i, pool_w16, pool_scale, mix_g, mix_b)
        else:
            ahead = (mlp_w1, mlp_w2, i + 1) if i + 1 < DEPTH else None
            proj_p, cast = _gla_proj(xp, j, w_in, ahead)
            if ahead:
                mlp_w[i + 1] = cast
            o_p, s_p, mlp_w[i] = _gla_rec_prompt(proj_p, batch, seq, j, wgu, gate_bias, norm_w, (mlp_w1, mlp_w2, i))
            gla_states_p.append(s_p)
            o_s, new_gla_sample = _gla_rec_sample(_gla_proj(xs, j, w_in)[0], state_gla, new_gla_sample, j, wgu,
                                                  gate_bias, norm_w)
            xp = _out_ln(o_p, xp, j, i, w_out, mix_g, mix_b)
            xs = _out_ln(o_s, xs, j, i, w_out, mix_g, mix_b)
        w1, w2 = mlp_w[i]
        xp = _mlp_ln(xp, i, w1, b1, w2, b2, ffn_g, ffn_b)
        xs = _mlp_ln(xs, i, w1, b1, w2, b2, ffn_g, ffn_b)

    new_rows = jnp.concatenate([state_rows[:, 1:], jnp.stack(pool_in_s)[:, None]], axis=1)
    new_pool_sample = jnp.transpose(new_rows, (0, 2, 1, 3))
    return (xp.reshape(batch, seq, D_MODEL), xs.reshape(bs, 1, D_MODEL), jnp.stack(pool_in_p),
            jnp.stack(gla_states_p), new_pool_sample, new_gla_sample)
```

```python
import functools

import jax
import jax.numpy as jnp
from jax import lax
from jax.experimental import pallas as pl
from jax.experimental.pallas import tpu as pltpu

F32 = jnp.float32
BF16 = jnp.bfloat16

D_MODEL = 2048
DEPTH = 4
PAST_LEN = 16384
POOL_WINDOWS = (2, 4, 8, 16)
POOL_GROUP = D_MODEL // len(POOL_WINDOWS)
POOL_CTX = max(POOL_WINDOWS) - 1
GLA_HEADS = 4
GLA_KEY_DIM = D_MODEL // 2
GLA_VAL_DIM = D_MODEL
GLA_DK = GLA_KEY_DIM // GLA_HEADS
GLA_DV = GLA_VAL_DIM // GLA_HEADS
GLA_GATE_RANK = 16
GLA_GATE_TEMP = 16.0
GLA_CHUNK = 16
D_FF = 4 * D_MODEL
LN_EPS = 1e-5
RMS_EPS = 1e-5
DN_ALPHA = (2 * DEPTH) ** 0.25
LOG2_E = 1.4426950408889634

LANES = 128
SUBLANES = 8
MXU_DIM = 256
VMEM_LIMIT_BYTES = 60 * 1024 * 1024

PROJ_QKVG = 2 * GLA_KEY_DIM + 2 * GLA_VAL_DIM
PROJ_TILE = 5 * MXU_DIM
PROJ_PAD = -(-(PROJ_QKVG + LANES) // PROJ_TILE) * PROJ_TILE
GLA_BLOCK = 128
CHUNKS_PER_BLOCK = GLA_BLOCK // GLA_CHUNK
POOL_ROWS = 256
MLP_ROWS = 256
OUT_ROWS = 128


def _params(*sem):
    return pltpu.CompilerParams(dimension_semantics=sem, vmem_limit_bytes=VMEM_LIMIT_BYTES)


def _layer_spec(layer, *block):
    return pl.BlockSpec((None,) + block, lambda *_: (layer,) + (0,) * len(block))


def _layer_norm(v, g, b):
    mu = jnp.mean(v, axis=-1, keepdims=True)
    c = v - mu
    var = jnp.mean(c * c, axis=-1, keepdims=True)
    return c * lax.rsqrt(var + LN_EPS) * g + b


def _split_bf16(z, parts):
    out = []
    for _ in range(parts - 1):
        hi = z.astype(BF16)
        out.append(hi)
        z = z - hi.astype(F32)
    out.append(z.astype(BF16))
    return out


def _mask_dot(mask_bf16, z, parts):
    terms = _split_bf16(z, parts)
    if mask_bf16.shape[1] % LANES == 0:
        return jnp.dot(jnp.concatenate([mask_bf16] * parts, axis=1), jnp.concatenate(terms, axis=0),
                       preferred_element_type=F32)
    return sum(jnp.dot(mask_bf16, t, preferred_element_type=F32) for t in terms)


def _log_sigmoid(x):
    return jnp.minimum(x, 0.0) - jnp.log(1.0 + jnp.exp(-jnp.abs(x)))


def _dot_nt(a, b):
    return lax.dot_general(a, b, (((1,), (1,)), ((), ())), preferred_element_type=F32)


def _dot_tn(a, b):
    return lax.dot_general(a, b, (((0,), (0,)), ((), ())), preferred_element_type=F32)


def _pool_ln_kernel(x_ref, halo_ref, bm_ref, bh_ref, w_ref, scale_ref, g_ref, b_ref, o_ref, *, tile_rows):
    j = pl.program_id(1)
    rb = min(POOL_ROWS, tile_rows)
    row = lax.broadcasted_iota(jnp.int32, (rb, 1), 0)
    for g, w in enumerate(POOL_WINDOWS):
        cols = slice(g * POOL_GROUP, (g + 1) * POOL_GROUP)
        for r in range(tile_rows // rb):
            xb = x_ref[r * rb:(r + 1) * rb, cols]
            if r == 0:
                halo = jnp.where(j == 0, 0.0, halo_ref[:, cols])
            else:
                halo = x_ref[r * rb - 16:r * rb, cols]
            win = _mask_dot(bm_ref[g], xb, 2)
            head = win[:16] + _mask_dot(bh_ref[g], halo, 2)
            win = jnp.concatenate([head, win[16:]], axis=0)
            count = jnp.minimum(j * tile_rows + r * rb + row + 1, w).astype(F32)
            p = win / count - xb
            y = jnp.dot(p.astype(BF16), w_ref[g], preferred_element_type=F32)
            o_ref[r * rb:(r + 1) * rb, cols] = y * scale_ref[:, cols]
    ln_rows = min(128, tile_rows)
    for r in range(tile_rows // ln_rows):
        rows = slice(r * ln_rows, (r + 1) * ln_rows)
        v = DN_ALPHA * x_ref[rows, :] + o_ref[rows, :]
        o_ref[rows, :] = _layer_norm(v, g_ref[...], b_ref[...])


def _band_matrices(rb):
    i = jnp.arange(rb)[:, None]
    jj = jnp.arange(rb)[None, :]
    bm = jnp.stack([((i - jj >= 0) & (i - jj < w)) for w in POOL_WINDOWS]).astype(BF16)
    i16 = jnp.arange(16)[:, None]
    j16 = jnp.arange(16)[None, :]
    bh = jnp.stack([(j16 >= i16 + 17 - w) for w in POOL_WINDOWS]).astype(BF16)
    return bm, bh


def _pool_ln_prompt(x2d, batch, seq, layer, mix_layer, w_all, scale_all, g_all, b_all):
    tile_rows = min(1024, seq)
    nt = seq // tile_rows
    bm, bh = _band_matrices(min(POOL_ROWS, tile_rows))
    const = lambda *shape: pl.BlockSpec(shape, lambda bi, j: (0,) * len(shape))
    return pl.pallas_call(
        functools.partial(_pool_ln_kernel, tile_rows=tile_rows),
        grid=(batch, nt),
        in_specs=[
            pl.BlockSpec((tile_rows, D_MODEL), lambda bi, j: (bi * nt + j, 0)),
            pl.BlockSpec((16, D_MODEL), lambda bi, j: (jnp.maximum((bi * nt + j) * (tile_rows // 16) - 1, 0), 0)),
            const(*bm.shape), const(*bh.shape),
            _layer_spec(layer, *w_all.shape[1:]),
            _layer_spec(layer, 1, D_MODEL), _layer_spec(mix_layer, 1, D_MODEL), _layer_spec(mix_layer, 1, D_MODEL),
        ],
        out_specs=pl.BlockSpec((tile_rows, D_MODEL), lambda bi, j: (bi * nt + j, 0)),
        out_shape=jax.ShapeDtypeStruct(x2d.shape, F32),
        compiler_params=_params("parallel", "arbitrary"),
        name="pool_ln_prompt",
    )(x2d, x2d, bm, bh, w_all, scale_all, g_all, b_all)


def _pool_ln_sample_kernel(x_ref, st_ref, w_ref, scale_ref, g_ref, b_ref, o_ref):
    x = x_ref[...]
    for g, w in enumerate(POOL_WINDOWS):
        cols = slice(g * POOL_GROUP, (g + 1) * POOL_GROUP)
        xg = x[:, cols]
        win = xg
        for r in range(POOL_CTX - (w - 1), POOL_CTX):
            win = win + st_ref[r, :, cols]
        p = win / float(min(PAST_LEN + 1, w)) - xg
        o_ref[:, cols] = jnp.dot(p.astype(BF16), w_ref[g], preferred_element_type=F32) * scale_ref[:, cols]
    o_ref[...] = _layer_norm(DN_ALPHA * x + o_ref[...], g_ref[...], b_ref[...])


def _pool_ln_sample(xs, state_rows, layer, mix_layer, w_all, scale_all, g_all, b_all):
    bs = xs.shape[0]
    tb = min(32, bs)
    return pl.pallas_call(
        _pool_ln_sample_kernel,
        grid=(bs // tb,),
        in_specs=[
            pl.BlockSpec((tb, D_MODEL), lambda i: (i, 0)),
            pl.BlockSpec((None, POOL_CTX, tb, D_MODEL), lambda i: (layer, 0, i, 0)),
            _layer_spec(layer, *w_all.shape[1:]),
            _layer_spec(layer, 1, D_MODEL), _layer_spec(mix_layer, 1, D_MODEL), _layer_spec(mix_layer, 1, D_MODEL),
        ],
        out_specs=pl.BlockSpec((tb, D_MODEL), lambda i: (i, 0)),
        out_shape=jax.ShapeDtypeStruct(xs.shape, F32),
        compiler_params=_params("parallel"),
        name="pool_ln_sample",
    )(xs, state_rows, w_all, scale_all, g_all, b_all)


def _mlp_ln_kernel(x_ref, w1_ref, b1_ref, w2_ref, b2_ref, g_ref, b_ref, o_ref, xb_ref):
    f = pl.program_id(1)

    @pl.when(f == 0)
    def _():
        xb_ref[...] = x_ref[...].astype(BF16)
        o_ref[...] = jnp.zeros_like(o_ref)

    tm = x_ref.shape[0]
    rb = min(MLP_ROWS, tm)

    def accumulate(finish):
        for r in range(tm // rb):
            rows = slice(r * rb, (r + 1) * rb)
            h = jnp.dot(xb_ref[rows, :], w1_ref[...], preferred_element_type=F32) + b1_ref[...]
            h = jnp.square(jnp.maximum(h, 0.0))
            acc = o_ref[rows, :] + jnp.dot(h.astype(BF16), w2_ref[...], preferred_element_type=F32)
            if finish:
                v = DN_ALPHA * x_ref[rows, :] + (acc + b2_ref[...])
                acc = _layer_norm(v, g_ref[...], b_ref[...])
            o_ref[rows, :] = acc

    last = pl.num_programs(1) - 1
    pl.when(f < last)(functools.partial(accumulate, False))
    pl.when(f == last)(functools.partial(accumulate, True))


def _mlp_ln(x2d, layer, w1, b1_all, w2, b2_all, g_all, b_all):
    m = x2d.shape[0]
    tm = min(1024, m)
    tf = 1024
    vec = lambda: pl.BlockSpec((None, 1, D_MODEL), lambda i, f: (layer, 0, 0))
    return pl.pallas_call(
        _mlp_ln_kernel,
        grid=(m // tm, D_FF // tf),
        in_specs=[
            pl.BlockSpec((tm, D_MODEL), lambda i, f: (i, 0)),
            pl.BlockSpec((D_MODEL, tf), lambda i, f: (0, f)),
            pl.BlockSpec((None, 1, tf), lambda i, f: (layer, 0, f)),
            pl.BlockSpec((tf, D_MODEL), lambda i, f: (f, 0)),
            vec(), vec(), vec(),
        ],
        out_specs=pl.BlockSpec((tm, D_MODEL), lambda i, f: (i, 0)),
        out_shape=jax.ShapeDtypeStruct(x2d.shape, F32),
        scratch_shapes=[pltpu.VMEM((tm, D_MODEL), BF16)],
        compiler_params=_params("parallel", "arbitrary"),
        name="mlp_ln",
    )(x2d, w1, b1_all, w2, b2_all, g_all, b_all)


class _CastJob:
    def __init__(self, w1_all, w2_all, layer, steps, step_of):
        nb = 1 << (steps.bit_length() - 1)
        slab = lambda *idx: jnp.minimum(step_of(*idx), nb - 1)
        r1, r2 = D_MODEL // nb, D_FF // nb
        self.args = [w1_all, w2_all]
        self.in_specs = [pl.BlockSpec((None, r1, D_FF), lambda *idx: (layer, slab(*idx), 0)),
                         pl.BlockSpec((None, r2, D_MODEL), lambda *idx: (layer, slab(*idx), 0))]
        self.out_specs = [pl.BlockSpec((r1, D_FF), lambda *idx: (slab(*idx), 0)),
                          pl.BlockSpec((r2, D_MODEL), lambda *idx: (slab(*idx), 0))]
        self.out_shape = [jax.ShapeDtypeStruct((D_MODEL, D_FF), BF16),
                          jax.ShapeDtypeStruct((D_FF, D_MODEL), BF16)]


def _cast_slabs(src1_ref, src2_ref, dst1_ref, dst2_ref):
    dst1_ref[...] = src1_ref[...].astype(BF16)
    dst2_ref[...] = src2_ref[...].astype(BF16)


def _proj_kernel(x_ref, w_ref, *rest, cast):
    if cast:
        src1_ref, src2_ref, o_ref, dst1_ref, dst2_ref, xb_ref = rest
        _cast_slabs(src1_ref, src2_ref, dst1_ref, dst2_ref)
    else:
        o_ref, xb_ref = rest

    @pl.when(pl.program_id(1) == 0)
    def _():
        xb_ref[...] = x_ref[...].astype(BF16)

    o_ref[...] = jnp.dot(xb_ref[...], w_ref[...], preferred_element_type=F32)


def _gla_proj(x2d, layer, w_in_all, cast_weights=None):
    m = x2d.shape[0]
    tm = min(1024, m)
    nn = PROJ_PAD // PROJ_TILE
    job = None
    if cast_weights is not None:
        job = _CastJob(*cast_weights, steps=(m // tm) * nn, step_of=lambda i, n: i * nn + n)
    outs = pl.pallas_call(
        functools.partial(_proj_kernel, cast=job is not None),
        grid=(m // tm, nn),
        in_specs=[
            pl.BlockSpec((tm, D_MODEL), lambda i, n: (i, 0)),
            pl.BlockSpec((None, D_MODEL, PROJ_TILE), lambda i, n: (layer, 0, n)),
        ] + (job.in_specs if job else []),
        out_specs=[pl.BlockSpec((tm, PROJ_TILE), lambda i, n: (i, n))] + (job.out_specs if job else []),
        out_shape=[jax.ShapeDtypeStruct((m, PROJ_PAD), F32)] + (job.out_shape if job else []),
        scratch_shapes=[pltpu.VMEM((tm, D_MODEL), BF16)],
        compiler_params=_params("arbitrary", "arbitrary"),
        name="gla_proj",
    )(x2d, w_in_all, *(job.args if job else []))
    return outs[0], tuple(outs[1:])


def _rms_gate(o, og, nw):
    o = o * lax.rsqrt(jnp.mean(o * o, axis=-1, keepdims=True) + RMS_EPS) * nw
    return o * (og * jax.nn.sigmoid(og))


def _gla_block(r0, q_ref, k_ref, v_ref, og_ref, nw_ref, o_ref, s_ref, bc_ref, a_ref):
    scale = GLA_DK ** -0.5
    half = GLA_CHUNK // 2
    chunks = range(CHUNKS_PER_BLOCK)
    blk = slice(r0, r0 + GLA_BLOCK)
    a_blk = a_ref.at[blk]

    tot = [bc_ref[r0 + (c + 1) * GLA_CHUNK - 1:r0 + (c + 1) * GLA_CHUNK, :] for c in chunks]
    before = [jnp.zeros_like(tot[0])]
    for c in chunks[1:]:
        before.append(before[-1] + tot[c - 1])
    after = [jnp.zeros_like(tot[0])]
    for c in reversed(chunks[:-1]):
        after.insert(0, after[0] + tot[c + 1])
    total = before[-1] + tot[-1]

    v = v_ref[blk, :].astype(BF16)
    qs, ks, bcs, qd, kd, q_hi, k_lo = [], [], [], [], [], [], []
    for c in chunks:
        rows = slice(r0 + c * GLA_CHUNK, r0 + (c + 1) * GLA_CHUNK)
        bcc = bc_ref[rows, :]
        qc = q_ref[rows, :] * scale
        kc = k_ref[rows, :]
        mid = bcc[half:half + 1]
        qs.append(qc)
        ks.append(kc)
        bcs.append(bcc)
        qd.append(qc * jnp.exp2(bcc))
        kd.append(kc * jnp.exp2(tot[c] - bcc))
        q_hi.append(qc[half:] * jnp.exp2(bcc[half:] - mid))
        k_lo.append(kc[:half] * jnp.exp2(mid - bcc[:half]))

    s0 = s_ref[...]
    q_in = jnp.concatenate([qd[c] * jnp.exp2(before[c]) for c in chunks], axis=0)
    o = jnp.dot(q_in.astype(BF16), s0.astype(BF16), preferred_element_type=F32)

    kd_all = jnp.concatenate(kd, axis=0).astype(BF16)
    lhs = []
    for jc in chunks[:-1]:
        lhs.append(qd[jc + 1])
        lhs.extend(qd[c] * jnp.exp2(before[c] - before[jc + 1]) for c in chunks[jc + 2:])
    pair = _dot_nt(jnp.concatenate(lhs, axis=0).astype(BF16), kd_all)
    a_blk[...] = jnp.zeros((GLA_BLOCK, GLA_BLOCK), F32)
    off = 0
    for jc in chunks[:-1]:
        lo = (jc + 1) * GLA_CHUNK
        n = GLA_BLOCK - lo
        lane = lax.broadcasted_iota(jnp.int32, (n, GLA_BLOCK), 1)
        keep = (lane >= jc * GLA_CHUNK) & (lane < lo)
        a_blk[lo:, :] += jnp.where(keep, pair[off:off + n], 0.0)
        off += n

    zeros = jnp.zeros((half, GLA_DK), F32)
    k_lo_all = jnp.concatenate([t for c in chunks for t in (k_lo[c], zeros)], axis=0).astype(BF16)
    cross = _dot_nt(jnp.concatenate(q_hi, axis=0).astype(BF16), k_lo_all)
    row = lax.broadcasted_iota(jnp.int32, (half, GLA_DK), 0)
    lane = lax.broadcasted_iota(jnp.int32, (half, GLA_BLOCK), 1)
    for c in chunks:
        for hs in range(2):
            rs = slice(hs * half, (hs + 1) * half)
            qh, kh, bh = qs[c][rs], ks[c][rs], bcs[c][rs]
            col0 = c * GLA_CHUNK + hs * half
            acc = jnp.zeros((half, GLA_BLOCK), F32)
            if hs == 1:
                in_chunk = (lane >= c * GLA_CHUNK) & (lane < col0)
                acc = jnp.where(in_chunk, cross[c * half:(c + 1) * half], 0.0)
            for j in range(half):
                d = bh - bh[j:j + 1]
                e = jnp.exp2(jnp.where(row >= j, d, -jnp.inf)) if j else jnp.exp2(d)
                t = qh * e * kh[j:j + 1]
                acc = acc + jnp.where(lane == col0 + j, jnp.sum(t, axis=1, keepdims=True), 0.0)
            a_blk[col0:col0 + half, :] += acc

    o = o + jnp.dot(a_blk[...].astype(BF16), v, preferred_element_type=F32)
    o_ref[blk, :] = _rms_gate(o, og_ref[blk, :], nw_ref[...]).astype(o_ref.dtype)

    k_out = jnp.concatenate([kd[c] * jnp.exp2(after[c]) for c in chunks], axis=0)
    upd = _dot_tn(k_out.astype(BF16), v)
    gcol = jnp.transpose(jnp.broadcast_to(jnp.exp2(total), (LANES, GLA_DK)))
    for n in range(GLA_DV // LANES):
        cols = slice(n * LANES, (n + 1) * LANES)
        s_ref[:, cols] = s0[:, cols] * gcol + upd[:, cols]


def _gla_rec_kernel(q_ref, k_ref, v_ref, og_ref, gl_ref, wgu_ref, gb_ref, nw_ref, tril_ref, *rest, cast):
    if cast:
        src1_ref, src2_ref, o_ref, sout_ref, dst1_ref, dst2_ref, s_ref, bc_ref, a_ref = rest
        _cast_slabs(src1_ref, src2_ref, dst1_ref, dst2_ref)
    else:
        o_ref, sout_ref, s_ref, bc_ref, a_ref = rest
    l = pl.program_id(2)

    @pl.when(l == 0)
    def _():
        s_ref[...] = jnp.zeros_like(s_ref)

    gk = jnp.dot(gl_ref[...].astype(BF16), wgu_ref[...], preferred_element_type=F32) + gb_ref[...]
    la = _log_sigmoid(gk) * (LOG2_E / GLA_GATE_TEMP)
    blocks = q_ref.shape[0] // GLA_BLOCK
    for i in range(blocks):
        blk = slice(i * GLA_BLOCK, (i + 1) * GLA_BLOCK)
        bc_ref[blk, :] = _mask_dot(tril_ref[...], la[blk], 2)
    for i in range(blocks):
        _gla_block(i * GLA_BLOCK, q_ref, k_ref, v_ref, og_ref, nw_ref, o_ref, s_ref, bc_ref, a_ref)

    @pl.when(l == pl.num_programs(2) - 1)
    def _():
        sout_ref[...] = s_ref[...]


def _chunk_tril():
    i = jnp.arange(GLA_BLOCK)[:, None]
    j = jnp.arange(GLA_BLOCK)[None, :]
    return ((i // GLA_CHUNK == j // GLA_CHUNK) & (j <= i)).astype(BF16)


def _gla_rec_prompt(proj, batch, seq, layer, wgu_all, gb_all, nw_all, cast_weights=None):
    tl = min(1024, seq)
    nl = seq // tl
    kq = GLA_KEY_DIM // GLA_DK
    kv = 2 * GLA_KEY_DIM // GLA_DV
    row = lambda b, h, l: b * nl + l
    job = None
    if cast_weights is not None:
        job = _CastJob(*cast_weights, steps=batch * GLA_HEADS * nl,
                       step_of=lambda b, h, l: (b * GLA_HEADS + h) * nl + l)
    outs = pl.pallas_call(
        functools.partial(_gla_rec_kernel, cast=job is not None),
        grid=(batch, GLA_HEADS, nl),
        in_specs=[
            pl.BlockSpec((tl, GLA_DK), lambda b, h, l: (row(b, h, l), h)),
            pl.BlockSpec((tl, GLA_DK), lambda b, h, l: (row(b, h, l), kq + h)),
            pl.BlockSpec((tl, GLA_DV), lambda b, h, l: (row(b, h, l), kv + h)),
            pl.BlockSpec((tl, GLA_DV), lambda b, h, l: (row(b, h, l), kv + GLA_HEADS + h)),
            pl.BlockSpec((tl, LANES), lambda b, h, l: (row(b, h, l), PROJ_QKVG // LANES)),
            pl.BlockSpec((None, LANES, GLA_DK), lambda b, h, l: (layer, 0, h)),
            pl.BlockSpec((None, 1, GLA_DK), lambda b, h, l: (layer, 0, h)),
            _layer_spec(layer, 1, GLA_DV),
            pl.BlockSpec((GLA_BLOCK, GLA_BLOCK), lambda b, h, l: (0, 0)),
        ] + (job.in_specs if job else []),
        out_specs=[
            pl.BlockSpec((tl, GLA_DV), lambda b, h, l: (row(b, h, l), h)),
            pl.BlockSpec((None, None, GLA_DK, GLA_DV), lambda b, h, l: (b, h, 0, 0)),
        ] + (job.out_specs if job else []),
        out_shape=[
            jax.ShapeDtypeStruct((batch * seq, GLA_VAL_DIM), BF16),
            jax.ShapeDtypeStruct((batch, GLA_HEADS, GLA_DK, GLA_DV), F32),
        ] + (job.out_shape if job else []),
        scratch_shapes=[
            pltpu.VMEM((GLA_DK, GLA_DV), F32),
            pltpu.VMEM((tl, GLA_DK), F32),
            pltpu.VMEM((tl, GLA_BLOCK), F32),
        ],
        compiler_params=_params("arbitrary", "arbitrary", "arbitrary"),
        name="gla_rec_prompt",
    )(proj, proj, proj, proj, proj, wgu_all, gb_all, nw_all, _chunk_tril(), *(job.args if job else []))
    return outs[0], outs[1], tuple(outs[2:])


def _gla_sample_kernel(q_ref, k_ref, v_ref, og_ref, gl_ref, s_ref, wgu_ref, gb_ref, nw_ref, *rest, tb):
    o_ref, sout_ref = rest[-2:]
    gk = jnp.dot(gl_ref[...].astype(BF16), wgu_ref[...], preferred_element_type=F32) + gb_ref[...]
    a = jnp.exp(_log_sigmoid(gk) / GLA_GATE_TEMP)
    q = q_ref[...] * (GLA_DK ** -0.5)
    k = k_ref[...]
    v = v_ref[...]
    qk = jnp.sum(q * k, axis=1, keepdims=True)
    pad = jnp.zeros((LANES - 3 * tb, GLA_DK), F32)
    cols = jnp.transpose(jnp.concatenate([a, k, q * a, pad], axis=0))
    outs = []
    for b in range(tb):
        s0 = s_ref[b]
        vb = v[b:b + 1]
        outs.append(qk[b:b + 1] * vb + jnp.sum(cols[:, 2 * tb + b:2 * tb + b + 1] * s0, axis=0, keepdims=True))
        sout_ref[b] = s0 * cols[:, b:b + 1] + cols[:, tb + b:tb + b + 1] * vb
    o_ref[...] = _rms_gate(jnp.concatenate(outs, axis=0), og_ref[...], nw_ref[...])


def _gla_rec_sample(proj, state_gla, new_state, layer, wgu_all, gb_all, nw_all):
    bs = proj.shape[0]
    tb = min(2 * SUBLANES, bs)
    kq = GLA_KEY_DIM // GLA_DK
    kv = 2 * GLA_KEY_DIM // GLA_DV
    s_spec = pl.BlockSpec((None, tb, None, GLA_DK, GLA_DV), lambda i, h: (layer, i, h, 0, 0))
    in_specs = [
        pl.BlockSpec((tb, GLA_DK), lambda i, h: (i, h)),
        pl.BlockSpec((tb, GLA_DK), lambda i, h: (i, kq + h)),
        pl.BlockSpec((tb, GLA_DV), lambda i, h: (i, kv + h)),
        pl.BlockSpec((tb, GLA_DV), lambda i, h: (i, kv + GLA_HEADS + h)),
        pl.BlockSpec((tb, LANES), lambda i, h: (i, PROJ_QKVG // LANES)),
        s_spec,
        pl.BlockSpec((None, LANES, GLA_DK), lambda i, h: (layer, 0, h)),
        pl.BlockSpec((None, 1, GLA_DK), lambda i, h: (layer, 0, h)),
        _layer_spec(layer, 1, GLA_DV),
    ]
    args = [proj, proj, proj, proj, proj, state_gla, wgu_all, gb_all, nw_all]
    aliases = {}
    if new_state is not None:
        in_specs.append(pl.BlockSpec(memory_space=pl.ANY))
        args.append(new_state)
        aliases = {len(args) - 1: 1}
    return pl.pallas_call(
        functools.partial(_gla_sample_kernel, tb=tb),
        grid=(bs // tb, GLA_HEADS),
        in_specs=in_specs,
        out_specs=[pl.BlockSpec((tb, GLA_DV), lambda i, h: (i, h)), s_spec],
        out_shape=[
            jax.ShapeDtypeStruct((bs, GLA_VAL_DIM), F32),
            jax.ShapeDtypeStruct(state_gla.shape, F32),
        ],
        input_output_aliases=aliases,
        compiler_params=_params("parallel", "parallel"),
        name="gla_rec_sample",
    )(*args)


def _out_ln_kernel(o_ref, w_ref, x_ref, g_ref, b_ref, y_ref):
    tm = x_ref.shape[0]
    rb = min(OUT_ROWS, tm)
    for r in range(tm // rb):
        rows = slice(r * rb, (r + 1) * rb)
        y = jnp.dot(o_ref[rows, :].astype(BF16), w_ref[...], preferred_element_type=F32)
        y_ref[rows, :] = _layer_norm(DN_ALPHA * x_ref[rows, :] + y, g_ref[...], b_ref[...])


def _out_ln(o2d, x2d, layer, mix_layer, w_out_all, g_all, b_all):
    m = x2d.shape[0]
    tm = min(512, m)
    return pl.pallas_call(
        _out_ln_kernel,
        grid=(m // tm,),
        in_specs=[
            pl.BlockSpec((tm, GLA_VAL_DIM), lambda i: (i, 0)),
            _layer_spec(layer, GLA_VAL_DIM, D_MODEL),
            pl.BlockSpec((tm, D_MODEL), lambda i: (i, 0)),
            _layer_spec(mix_layer, 1, D_MODEL), _layer_spec(mix_layer, 1, D_MODEL),
        ],
        out_specs=pl.BlockSpec((tm, D_MODEL), lambda i: (i, 0)),
        out_shape=jax.ShapeDtypeStruct(x2d.shape, F32),
        compiler_params=_params("parallel"),
        name="gla_out_ln",
    )(o2d, w_out_all, x2d, g_all, b_all)


def kernel(x_prompt, x_sample, state_pool, state_gla, pool_w, pool_scale, gla_w_in, gla_w_gate_up,
           gla_gate_bias, gla_norm_w, gla_w_out, ln_mix_g, ln_mix_b, mlp_w1, mlp_b1, mlp_w2, mlp_b2,
           ln_ffn_g, ln_ffn_b):
    batch, seq, _ = x_prompt.shape
    bs = x_sample.shape[0]
    xp = x_prompt.reshape(batch * seq, D_MODEL)
    xs = x_sample.reshape(bs, D_MODEL)

    vec = lambda a: a.reshape(a.shape[0], 1, a.shape[1])
    pool_w16, w_out = pool_w.astype(BF16), gla_w_out.astype(BF16)
    mlp_w = {0: (mlp_w1[0].astype(BF16), mlp_w2[0].astype(BF16))}
    w_in = jnp.pad(gla_w_in.astype(BF16), ((0, 0), (0, 0), (0, PROJ_PAD - gla_w_in.shape[-1])))
    wgu = jnp.pad(gla_w_gate_up.astype(BF16), ((0, 0), (0, LANES - GLA_GATE_RANK), (0, 0)))
    pool_scale, gate_bias, norm_w, mix_g, mix_b, b1, b2, ffn_g, ffn_b = map(
        vec, (pool_scale, gla_gate_bias, gla_norm_w, ln_mix_g, ln_mix_b, mlp_b1, mlp_b2, ln_ffn_g, ln_ffn_b))

    state_rows = jnp.transpose(state_pool, (0, 2, 1, 3))
    pool_in_p, pool_in_s, gla_states_p = [], [], []
    new_gla_sample = None
    for i in range(DEPTH):
        j = i // 2
        if i % 2 == 0:
            pool_in_p.append(xp.reshape(batch, seq, D_MODEL)[:, seq - POOL_CTX:])
            pool_in_s.append(xs)
            xp = _pool_ln_prompt(xp, batch, seq, j, i, pool_w16, pool_scale, mix_g, mix_b)
            xs = _pool_ln_sample(xs, state_rows, j, i, pool_w16, pool_scale, mix_g, mix_b)
        else:
            ahead = (mlp_w1, mlp_w2, i + 1) if i + 1 < DEPTH else None
            proj_p, cast = _gla_proj(xp, j, w_in, ahead)
            if ahead:
                mlp_w[i + 1] = cast
            o_p, s_p, mlp_w[i] = _gla_rec_prompt(proj_p, batch, seq, j, wgu, gate_bias, norm_w, (mlp_w1, mlp_w2, i))
            gla_states_p.append(s_p)
            o_s, new_gla_sample = _gla_rec_sample(_gla_proj(xs, j, w_in)[0], state_gla, new_gla_sample, j, wgu,
                                                  gate_bias, norm_w)
            xp = _out_ln(o_p, xp, j, i, w_out, mix_g, mix_b)
            xs = _out_ln(o_s, xs, j, i, w_out, mix_g, mix_b)
        w1, w2 = mlp_w[i]
        xp = _mlp_ln(xp, i, w1, b1, w2, b2, ffn_g, ffn_b)
        xs = _mlp_ln(xs, i, w1, b1, w2, b2, ffn_g, ffn_b)

    new_rows = jnp.concatenate([state_rows[:, 1:], jnp.stack(pool_in_s)[:, None]], axis=1)
    new_pool_sample = jnp.transpose(new_rows, (0, 2, 1, 3))
    return (xp.reshape(batch, seq, D_MODEL), xs.reshape(bs, 1, D_MODEL), jnp.stack(pool_in_p),
            jnp.stack(gla_states_p), new_pool_sample, new_gla_sample)
```

```python
import functools

import jax
import jax.numpy as jnp
from jax import lax
from jax.experimental import pallas as pl
from jax.experimental.pallas import tpu as pltpu

F32 = jnp.float32
BF16 = jnp.bfloat16

D_MODEL = 2048
DEPTH = 4
PAST_LEN = 16384
POOL_WINDOWS = (2, 4, 8, 16)
POOL_GROUP = D_MODEL // len(POOL_WINDOWS)
POOL_CTX = max(POOL_WINDOWS) - 1
GLA_HEADS = 4
GLA_KEY_DIM = D_MODEL // 2
GLA_VAL_DIM = D_MODEL
GLA_DK = GLA_KEY_DIM // GLA_HEADS
GLA_DV = GLA_VAL_DIM // GLA_HEADS
GLA_GATE_RANK = 16
GLA_GATE_TEMP = 16.0
GLA_CHUNK = 16
D_FF = 4 * D_MODEL
LN_EPS = 1e-5
RMS_EPS = 1e-5
DN_ALPHA = (2 * DEPTH) ** 0.25
LOG2_E = 1.4426950408889634

LANES = 128
SUBLANES = 8
MXU_DIM = 256
VMEM_LIMIT_BYTES = 60 * 1024 * 1024

PROJ_QKVG = 2 * GLA_KEY_DIM + 2 * GLA_VAL_DIM
PROJ_TILE = 5 * MXU_DIM
PROJ_PAD = -(-(PROJ_QKVG + LANES) // PROJ_TILE) * PROJ_TILE
GLA_BLOCK = 128
CHUNKS_PER_BLOCK = GLA_BLOCK // GLA_CHUNK
POOL_ROWS = 256
MLP_ROWS = 256
OUT_ROWS = 128


def _params(*sem):
    return pltpu.CompilerParams(dimension_semantics=sem, vmem_limit_bytes=VMEM_LIMIT_BYTES)


def _layer_spec(layer, *block):
    return pl.BlockSpec((None,) + block, lambda *_: (layer,) + (0,) * len(block))


def _layer_norm(v, g, b):
    mu = jnp.mean(v, axis=-1, keepdims=True)
    c = v - mu
    var = jnp.mean(c * c, axis=-1, keepdims=True)
    return c * lax.rsqrt(var + LN_EPS) * g + b


def _split_bf16(z, parts):
    out = []
    for _ in range(parts - 1):
        hi = z.astype(BF16)
        out.append(hi)
        z = z - hi.astype(F32)
    out.append(z.astype(BF16))
    return out


def _mask_dot(mask_bf16, z, parts):
    terms = _split_bf16(z, parts)
    if mask_bf16.shape[1] % LANES == 0:
        return jnp.dot(jnp.concatenate([mask_bf16] * parts, axis=1), jnp.concatenate(terms, axis=0),
                       preferred_element_type=F32)
    return sum(jnp.dot(mask_bf16, t, preferred_element_type=F32) for t in terms)


def _log_sigmoid(x):
    return jnp.minimum(x, 0.0) - jnp.log(1.0 + jnp.exp(-jnp.abs(x)))


def _dot_nt(a, b):
    return lax.dot_general(a, b, (((1,), (1,)), ((), ())), preferred_element_type=F32)


def _dot_tn(a, b):
    return lax.dot_general(a, b, (((0,), (0,)), ((), ())), preferred_element_type=F32)


def _pool_ln_kernel(x_ref, halo_ref, bm_ref, bh_ref, w_ref, scale_ref, g_ref, b_ref, o_ref, *, tile_rows):
    j = pl.program_id(1)
    rb = min(POOL_ROWS, tile_rows)
    row = lax.broadcasted_iota(jnp.int32, (rb, 1), 0)
    for g, w in enumerate(POOL_WINDOWS):
        cols = slice(g * POOL_GROUP, (g + 1) * POOL_GROUP)
        for r in range(tile_rows // rb):
            xb = x_ref[r * rb:(r + 1) * rb, cols]
            if r == 0:
                halo = jnp.where(j == 0, 0.0, halo_ref[:, cols])
            else:
                halo = x_ref[r * rb - 16:r * rb, cols]
            win = _mask_dot(bm_ref[g], xb, 2)
            head = win[:16] + _mask_dot(bh_ref[g], halo, 2)
            win = jnp.concatenate([head, win[16:]], axis=0)
            count = jnp.minimum(j * tile_rows + r * rb + row + 1, w).astype(F32)
            p = win / count - xb
            y = jnp.dot(p.astype(BF16), w_ref[g].astype(BF16), preferred_element_type=F32)
            o_ref[r * rb:(r + 1) * rb, cols] = y * scale_ref[:, cols]
    ln_rows = min(128, tile_rows)
    for r in range(tile_rows // ln_rows):
        rows = slice(r * ln_rows, (r + 1) * ln_rows)
        v = DN_ALPHA * x_ref[rows, :] + o_ref[rows, :]
        o_ref[rows, :] = _layer_norm(v, g_ref[...], b_ref[...])


def _band_matrices(rb):
    i = jnp.arange(rb)[:, None]
    jj = jnp.arange(rb)[None, :]
    bm = jnp.stack([((i - jj >= 0) & (i - jj < w)) for w in POOL_WINDOWS]).astype(BF16)
    i16 = jnp.arange(16)[:, None]
    j16 = jnp.arange(16)[None, :]
    bh = jnp.stack([(j16 >= i16 + 17 - w) for w in POOL_WINDOWS]).astype(BF16)
    return bm, bh


def _pool_ln_prompt(x2d, batch, seq, layer, mix_layer, w_all, scale_all, g_all, b_all):
    tile_rows = min(1024, seq)
    nt = seq // tile_rows
    bm, bh = _band_matrices(min(POOL_ROWS, tile_rows))
    const = lambda *shape: pl.BlockSpec(shape, lambda bi, j: (0,) * len(shape))
    return pl.pallas_call(
        functools.partial(_pool_ln_kernel, tile_rows=tile_rows),
        grid=(batch, nt),
        in_specs=[
            pl.BlockSpec((tile_rows, D_MODEL), lambda bi, j: (bi * nt + j, 0)),
            pl.BlockSpec((16, D_MODEL), lambda bi, j: (jnp.maximum((bi * nt + j) * (tile_rows // 16) - 1, 0), 0)),
            const(*bm.shape), const(*bh.shape),
            _layer_spec(layer, *w_all.shape[1:]),
            _layer_spec(layer, 1, D_MODEL), _layer_spec(mix_layer, 1, D_MODEL), _layer_spec(mix_layer, 1, D_MODEL),
        ],
        out_specs=pl.BlockSpec((tile_rows, D_MODEL), lambda bi, j: (bi * nt + j, 0)),
        out_shape=jax.ShapeDtypeStruct(x2d.shape, F32),
        compiler_params=_params("parallel", "arbitrary"),
        name="pool_ln_prompt",
    )(x2d, x2d, bm, bh, w_all, scale_all, g_all, b_all)


def _pool_ln_sample_kernel(x_ref, st_ref, w_ref, scale_ref, g_ref, b_ref, *rest):
    o_ref, new_ref = rest[-2:]
    x = x_ref[...]
    new_ref[:POOL_CTX - 1] = st_ref[1:]
    new_ref[POOL_CTX - 1] = x
    for g, w in enumerate(POOL_WINDOWS):
        cols = slice(g * POOL_GROUP, (g + 1) * POOL_GROUP)
        xg = x[:, cols]
        win = xg
        for r in range(POOL_CTX - (w - 1), POOL_CTX):
            win = win + st_ref[r, :, cols]
        p = win / float(min(PAST_LEN + 1, w)) - xg
        y = jnp.dot(p.astype(BF16), w_ref[g].astype(BF16), preferred_element_type=F32)
        o_ref[:, cols] = y * scale_ref[:, cols]
    o_ref[...] = _layer_norm(DN_ALPHA * x + o_ref[...], g_ref[...], b_ref[...])


def _pool_ln_sample(xs, state_rows, new_rows, layer, mix_layer, w_all, scale_all, g_all, b_all):
    bs = xs.shape[0]
    tb = min(32, bs)
    st_spec = pl.BlockSpec((None, POOL_CTX, tb, D_MODEL), lambda i: (layer, 0, i, 0))
    in_specs = [
        pl.BlockSpec((tb, D_MODEL), lambda i: (i, 0)),
        st_spec,
        _layer_spec(layer, *w_all.shape[1:]),
        _layer_spec(layer, 1, D_MODEL), _layer_spec(mix_layer, 1, D_MODEL), _layer_spec(mix_layer, 1, D_MODEL),
    ]
    args = [xs, state_rows, w_all, scale_all, g_all, b_all]
    aliases = {}
    if new_rows is not None:
        in_specs.append(pl.BlockSpec(memory_space=pl.ANY))
        args.append(new_rows)
        aliases = {len(args) - 1: 1}
    return pl.pallas_call(
        _pool_ln_sample_kernel,
        grid=(bs // tb,),
        in_specs=in_specs,
        out_specs=[pl.BlockSpec((tb, D_MODEL), lambda i: (i, 0)), st_spec],
        out_shape=[jax.ShapeDtypeStruct(xs.shape, F32), jax.ShapeDtypeStruct(state_rows.shape, F32)],
        input_output_aliases=aliases,
        compiler_params=_params("parallel"),
        name="pool_ln_sample",
    )(*args)


def _mlp_ln_kernel(x_ref, *rest, sample_rows):
    if sample_rows:
        xs_ref, w1_ref, b1_ref, w2_ref, b2_ref, g_ref, b_ref, o_ref, os_ref, xb_ref = rest
    else:
        w1_ref, b1_ref, w2_ref, b2_ref, g_ref, b_ref, o_ref, xb_ref = rest
    f = pl.program_id(1)
    tm = x_ref.shape[0]
    total = tm + sample_rows

    @pl.when(f == 0)
    def _():
        xb_ref[:tm, :] = x_ref[...].astype(BF16)
        o_ref[...] = jnp.zeros_like(o_ref)
        if sample_rows:
            xb_ref[tm:, :] = xs_ref[...].astype(BF16)
            os_ref[...] = jnp.zeros_like(os_ref)

    chunks = max(tm // MLP_ROWS, 1) + (1 if sample_rows else 0)
    rb = total // chunks

    def accumulate(finish):
        for c in range(chunks):
            lo, hi = c * rb, (c + 1) * rb
            h = jnp.dot(xb_ref[lo:hi, :], w1_ref[...], preferred_element_type=F32) + b1_ref[...]
            h = jnp.square(jnp.maximum(h, 0.0))
            part = jnp.dot(h.astype(BF16), w2_ref[...], preferred_element_type=F32)
            pieces = []
            if lo < tm:
                pieces.append((x_ref, o_ref, lo, min(hi, tm), 0))
            if hi > tm:
                pieces.append((xs_ref, os_ref, max(lo, tm) - tm, hi - tm, max(lo, tm) - lo))
            for src, dst, a, b, off in pieces:
                acc = dst[a:b, :] + part[off:off + b - a]
                if finish:
                    v = DN_ALPHA * src[a:b, :] + (acc + b2_ref[...])
                    acc = _layer_norm(v, g_ref[...], b_ref[...])
                dst[a:b, :] = acc

    last = pl.num_programs(1) - 1
    pl.when(f < last)(functools.partial(accumulate, False))
    pl.when(f == last)(functools.partial(accumulate, True))


def _mlp_ln(x2d, xs2d, layer, w1, b1_all, w2, b2_all, g_all, b_all):
    m, bs = x2d.shape[0], xs2d.shape[0]
    tm = min(1024, m)
    tiles = m // tm
    ts = bs // tiles
    chunks = max(tm // MLP_ROWS, 1) + 1
    if bs % tiles or ts % 16 or (tm + ts) % chunks or ((tm + ts) // chunks) % 16:
        return tuple(_mlp_ln_call(a, None, 0, layer, w1, b1_all, w2, b2_all, g_all, b_all)[0] for a in (x2d, xs2d))
    return _mlp_ln_call(x2d, xs2d, ts, layer, w1, b1_all, w2, b2_all, g_all, b_all)


def _mlp_ln_call(x2d, xs2d, ts, layer, w1, b1_all, w2, b2_all, g_all, b_all):
    m = x2d.shape[0]
    tm = min(1024, m)
    tf = 1024
    vec = lambda: pl.BlockSpec((None, 1, D_MODEL), lambda i, f: (layer, 0, 0))
    rows = lambda n: pl.BlockSpec((n, D_MODEL), lambda i, f: (i, 0))
    sample = [xs2d] if ts else []
    return pl.pallas_call(
        functools.partial(_mlp_ln_kernel, sample_rows=ts),
        grid=(m // tm, D_FF // tf),
        in_specs=[rows(tm)] + [rows(ts)] * bool(ts) + [
            pl.BlockSpec((D_MODEL, tf), lambda i, f: (0, f)),
            pl.BlockSpec((None, 1, tf), lambda i, f: (layer, 0, f)),
            pl.BlockSpec((tf, D_MODEL), lambda i, f: (f, 0)),
            vec(), vec(), vec(),
        ],
        out_specs=[rows(tm)] + [rows(ts)] * bool(ts),
        out_shape=[jax.ShapeDtypeStruct(a.shape, F32) for a in [x2d] + sample],
        scratch_shapes=[pltpu.VMEM((tm + ts, D_MODEL), BF16)],
        compiler_params=_params("parallel", "arbitrary"),
        name="mlp_ln",
    )(x2d, *sample, w1, b1_all, w2, b2_all, g_all, b_all)


class _CastJob:
    def __init__(self, w1_all, w2_all, layer, steps, step_of):
        nb = 1 << (steps.bit_length() - 1)
        slab = lambda *idx: jnp.minimum(step_of(*idx), nb - 1)
        r1, r2 = D_MODEL // nb, D_FF // nb
        self.args = [w1_all, w2_all]
        self.in_specs = [pl.BlockSpec((None, r1, D_FF), lambda *idx: (layer, slab(*idx), 0)),
                         pl.BlockSpec((None, r2, D_MODEL), lambda *idx: (layer, slab(*idx), 0))]
        self.out_specs = [pl.BlockSpec((r1, D_FF), lambda *idx: (slab(*idx), 0)),
                          pl.BlockSpec((r2, D_MODEL), lambda *idx: (slab(*idx), 0))]
        self.out_shape = [jax.ShapeDtypeStruct((D_MODEL, D_FF), BF16),
                          jax.ShapeDtypeStruct((D_FF, D_MODEL), BF16)]


def _cast_slabs(src1_ref, src2_ref, dst1_ref, dst2_ref):
    dst1_ref[...] = src1_ref[...].astype(BF16)
    dst2_ref[...] = src2_ref[...].astype(BF16)


def _proj_kernel(x_ref, w_ref, *rest, cast):
    if cast:
        src1_ref, src2_ref, o_ref, dst1_ref, dst2_ref, xb_ref = rest
        _cast_slabs(src1_ref, src2_ref, dst1_ref, dst2_ref)
    else:
        o_ref, xb_ref = rest

    @pl.when(pl.program_id(1) == 0)
    def _():
        xb_ref[...] = x_ref[...].astype(BF16)

    o_ref[...] = jnp.dot(xb_ref[...], w_ref[...], preferred_element_type=F32)


def _gla_proj(x2d, layer, w_in_all, cast_weights=None):
    m = x2d.shape[0]
    tm = min(1024, m)
    nn = PROJ_PAD // PROJ_TILE
    job = None
    if cast_weights is not None:
        job = _CastJob(*cast_weights, steps=(m // tm) * nn, step_of=lambda i, n: i * nn + n)
    outs = pl.pallas_call(
        functools.partial(_proj_kernel, cast=job is not None),
        grid=(m // tm, nn),
        in_specs=[
            pl.BlockSpec((tm, D_MODEL), lambda i, n: (i, 0)),
            pl.BlockSpec((None, D_MODEL, PROJ_TILE), lambda i, n: (layer, 0, n)),
        ] + (job.in_specs if job else []),
        out_specs=[pl.BlockSpec((tm, PROJ_TILE), lambda i, n: (i, n))] + (job.out_specs if job else []),
        out_shape=[jax.ShapeDtypeStruct((m, PROJ_PAD), F32)] + (job.out_shape if job else []),
        scratch_shapes=[pltpu.VMEM((tm, D_MODEL), BF16)],
        compiler_params=_params("arbitrary", "arbitrary"),
        name="gla_proj",
    )(x2d, w_in_all, *(job.args if job else []))
    return outs[0], tuple(outs[1:])


def _rms_gate(o, og, nw):
    o = o * lax.rsqrt(jnp.mean(o * o, axis=-1, keepdims=True) + RMS_EPS) * nw
    return o * (og * jax.nn.sigmoid(og))


def _gla_block(r0, q_ref, k_ref, v_ref, og_ref, nw_ref, o_ref, s_ref, bc_ref, a_ref):
    scale = GLA_DK ** -0.5
    half = GLA_CHUNK // 2
    chunks = range(CHUNKS_PER_BLOCK)
    blk = slice(r0, r0 + GLA_BLOCK)
    a_blk = a_ref.at[blk]

    tot = [bc_ref[r0 + (c + 1) * GLA_CHUNK - 1:r0 + (c + 1) * GLA_CHUNK, :] for c in chunks]
    before = [jnp.zeros_like(tot[0])]
    for c in chunks[1:]:
        before.append(before[-1] + tot[c - 1])
    after = [jnp.zeros_like(tot[0])]
    for c in reversed(chunks[:-1]):
        after.insert(0, after[0] + tot[c + 1])
    total = before[-1] + tot[-1]

    v = v_ref[blk, :].astype(BF16)
    qs, ks, bcs, qd, kd, q_hi, k_lo = [], [], [], [], [], [], []
    for c in chunks:
        rows = slice(r0 + c * GLA_CHUNK, r0 + (c + 1) * GLA_CHUNK)
        bcc = bc_ref[rows, :]
        qc = q_ref[rows, :] * scale
        kc = k_ref[rows, :]
        mid = bcc[half:half + 1]
        qs.append(qc)
        ks.append(kc)
        bcs.append(bcc)
        qd.append(qc * jnp.exp2(bcc))
        kd.append(kc * jnp.exp2(tot[c] - bcc))
        q_hi.append(qc[half:] * jnp.exp2(bcc[half:] - mid))
        k_lo.append(kc[:half] * jnp.exp2(mid - bcc[:half]))

    s0 = s_ref[...]
    q_in = jnp.concatenate([qd[c] * jnp.exp2(before[c]) for c in chunks], axis=0)
    o = jnp.dot(q_in.astype(BF16), s0.astype(BF16), preferred_element_type=F32)

    kd_all = jnp.concatenate(kd, axis=0).astype(BF16)
    lhs = []
    for jc in chunks[:-1]:
        lhs.append(qd[jc + 1])
        lhs.extend(qd[c] * jnp.exp2(before[c] - before[jc + 1]) for c in chunks[jc + 2:])
    pair = _dot_nt(jnp.concatenate(lhs, axis=0).astype(BF16), kd_all)
    a_blk[...] = jnp.zeros((GLA_BLOCK, GLA_BLOCK), F32)
    off = 0
    for jc in chunks[:-1]:
        lo = (jc + 1) * GLA_CHUNK
        n = GLA_BLOCK - lo
        lane = lax.broadcasted_iota(jnp.int32, (n, GLA_BLOCK), 1)
        keep = (lane >= jc * GLA_CHUNK) & (lane < lo)
        a_blk[lo:, :] += jnp.where(keep, pair[off:off + n], 0.0)
        off += n

    zeros = jnp.zeros((half, GLA_DK), F32)
    k_lo_all = jnp.concatenate([t for c in chunks for t in (k_lo[c], zeros)], axis=0).astype(BF16)
    cross = _dot_nt(jnp.concatenate(q_hi, axis=0).astype(BF16), k_lo_all)
    row = lax.broadcasted_iota(jnp.int32, (half, GLA_DK), 0)
    lane = lax.broadcasted_iota(jnp.int32, (half, GLA_BLOCK), 1)
    for c in chunks:
        for hs in range(2):
            rs = slice(hs * half, (hs + 1) * half)
            qh, kh, bh = qs[c][rs], ks[c][rs], bcs[c][rs]
            col0 = c * GLA_CHUNK + hs * half
            acc = jnp.zeros((half, GLA_BLOCK), F32)
            if hs == 1:
                in_chunk = (lane >= c * GLA_CHUNK) & (lane < col0)
                acc = jnp.where(in_chunk, cross[c * half:(c + 1) * half], 0.0)
            for j in range(half):
                d = bh - bh[j:j + 1]
                e = jnp.exp2(jnp.where(row >= j, d, -jnp.inf)) if j else jnp.exp2(d)
                t = qh * e * kh[j:j + 1]
                acc = acc + jnp.where(lane == col0 + j, jnp.sum(t, axis=1, keepdims=True), 0.0)
            a_blk[col0:col0 + half, :] += acc

    o = o + jnp.dot(a_blk[...].astype(BF16), v, preferred_element_type=F32)
    o_ref[blk, :] = _rms_gate(o, og_ref[blk, :], nw_ref[...]).astype(o_ref.dtype)

    k_out = jnp.concatenate([kd[c] * jnp.exp2(after[c]) for c in chunks], axis=0)
    upd = _dot_tn(k_out.astype(BF16), v)
    gcol = jnp.transpose(jnp.broadcast_to(jnp.exp2(total), (LANES, GLA_DK)))
    for n in range(GLA_DV // LANES):
        cols = slice(n * LANES, (n + 1) * LANES)
        s_ref[:, cols] = s0[:, cols] * gcol + upd[:, cols]


def _gla_rec_kernel(q_ref, k_ref, v_ref, og_ref, gl_ref, wgu_ref, gb_ref, nw_ref, tril_ref, *rest, cast):
    if cast:
        src1_ref, src2_ref, o_ref, sout_ref, dst1_ref, dst2_ref, s_ref, bc_ref, a_ref = rest
        _cast_slabs(src1_ref, src2_ref, dst1_ref, dst2_ref)
    else:
        o_ref, sout_ref, s_ref, bc_ref, a_ref = rest
    l = pl.program_id(2)

    @pl.when(l == 0)
    def _():
        s_ref[...] = jnp.zeros_like(s_ref)

    gk = jnp.dot(gl_ref[...].astype(BF16), wgu_ref[...], preferred_element_type=F32) + gb_ref[...]
    la = _log_sigmoid(gk) * (LOG2_E / GLA_GATE_TEMP)
    blocks = q_ref.shape[0] // GLA_BLOCK
    for i in range(blocks):
        blk = slice(i * GLA_BLOCK, (i + 1) * GLA_BLOCK)
        bc_ref[blk, :] = _mask_dot(tril_ref[...], la[blk], 2)
    for i in range(blocks):
        _gla_block(i * GLA_BLOCK, q_ref, k_ref, v_ref, og_ref, nw_ref, o_ref, s_ref, bc_ref, a_ref)

    @pl.when(l == pl.num_programs(2) - 1)
    def _():
        sout_ref[...] = s_ref[...]


def _chunk_tril():
    i = jnp.arange(GLA_BLOCK)[:, None]
    j = jnp.arange(GLA_BLOCK)[None, :]
    return ((i // GLA_CHUNK == j // GLA_CHUNK) & (j <= i)).astype(BF16)


def _gla_rec_prompt(proj, batch, seq, layer, wgu_all, gb_all, nw_all, cast_weights=None):
    tl = min(1024, seq)
    nl = seq // tl
    kq = GLA_KEY_DIM // GLA_DK
    kv = 2 * GLA_KEY_DIM // GLA_DV
    row = lambda b, h, l: b * nl + l
    job = None
    if cast_weights is not None:
        job = _CastJob(*cast_weights, steps=batch * GLA_HEADS * nl,
                       step_of=lambda b, h, l: (b * GLA_HEADS + h) * nl + l)
    outs = pl.pallas_call(
        functools.partial(_gla_rec_kernel, cast=job is not None),
        grid=(batch, GLA_HEADS, nl),
        in_specs=[
            pl.BlockSpec((tl, GLA_DK), lambda b, h, l: (row(b, h, l), h)),
            pl.BlockSpec((tl, GLA_DK), lambda b, h, l: (row(b, h, l), kq + h)),
            pl.BlockSpec((tl, GLA_DV), lambda b, h, l: (row(b, h, l), kv + h)),
            pl.BlockSpec((tl, GLA_DV), lambda b, h, l: (row(b, h, l), kv + GLA_HEADS + h)),
            pl.BlockSpec((tl, LANES), lambda b, h, l: (row(b, h, l), PROJ_QKVG // LANES)),
            pl.BlockSpec((None, LANES, GLA_DK), lambda b, h, l: (layer, 0, h)),
            pl.BlockSpec((None, 1, GLA_DK), lambda b, h, l: (layer, 0, h)),
            _layer_spec(layer, 1, GLA_DV),
            pl.BlockSpec((GLA_BLOCK, GLA_BLOCK), lambda b, h, l: (0, 0)),
        ] + (job.in_specs if job else []),
        out_specs=[
            pl.BlockSpec((tl, GLA_DV), lambda b, h, l: (row(b, h, l), h)),
            pl.BlockSpec((None, None, GLA_DK, GLA_DV), lambda b, h, l: (b, h, 0, 0)),
        ] + (job.out_specs if job else []),
        out_shape=[
            jax.ShapeDtypeStruct((batch * seq, GLA_VAL_DIM), BF16),
            jax.ShapeDtypeStruct((batch, GLA_HEADS, GLA_DK, GLA_DV), F32),
        ] + (job.out_shape if job else []),
        scratch_shapes=[
            pltpu.VMEM((GLA_DK, GLA_DV), F32),
            pltpu.VMEM((tl, GLA_DK), F32),
            pltpu.VMEM((tl, GLA_BLOCK), F32),
        ],
        compiler_params=_params("arbitrary", "arbitrary", "arbitrary"),
        name="gla_rec_prompt",
    )(proj, proj, proj, proj, proj, wgu_all, gb_all, nw_all, _chunk_tril(), *(job.args if job else []))
    return outs[0], outs[1], tuple(outs[2:])


def _gla_sample_kernel(q_ref, k_ref, v_ref, og_ref, gl_ref, s_ref, wgu_ref, gb_ref, nw_ref, *rest, tb):
    o_ref, sout_ref = rest[-2:]
    gk = jnp.dot(gl_ref[...].astype(BF16), wgu_ref[...], preferred_element_type=F32) + gb_ref[...]
    a = jnp.exp(_log_sigmoid(gk) / GLA_GATE_TEMP)
    q = q_ref[...] * (GLA_DK ** -0.5)
    k = k_ref[...]
    v = v_ref[...]
    qk = jnp.sum(q * k, axis=1, keepdims=True)
    pad = jnp.zeros((LANES - 3 * tb, GLA_DK), F32)
    cols = jnp.transpose(jnp.concatenate([a, k, q * a, pad], axis=0))
    outs = []
    for b in range(tb):
        s0 = s_ref[b]
        vb = v[b:b + 1]
        outs.append(qk[b:b + 1] * vb + jnp.sum(cols[:, 2 * tb + b:2 * tb + b + 1] * s0, axis=0, keepdims=True))
        sout_ref[b] = s0 * cols[:, b:b + 1] + cols[:, tb + b:tb + b + 1] * vb
    o_ref[...] = _rms_gate(jnp.concatenate(outs, axis=0), og_ref[...], nw_ref[...])


def _gla_rec_sample(proj, state_gla, new_state, layer, wgu_all, gb_all, nw_all):
    bs = proj.shape[0]
    tb = min(2 * SUBLANES, bs)
    kq = GLA_KEY_DIM // GLA_DK
    kv = 2 * GLA_KEY_DIM // GLA_DV
    s_spec = pl.BlockSpec((None, tb, None, GLA_DK, GLA_DV), lambda i, h: (layer, i, h, 0, 0))
    in_specs = [
        pl.BlockSpec((tb, GLA_DK), lambda i, h: (i, h)),
        pl.BlockSpec((tb, GLA_DK), lambda i, h: (i, kq + h)),
        pl.BlockSpec((tb, GLA_DV), lambda i, h: (i, kv + h)),
        pl.BlockSpec((tb, GLA_DV), lambda i, h: (i, kv + GLA_HEADS + h)),
        pl.BlockSpec((tb, LANES), lambda i, h: (i, PROJ_QKVG // LANES)),
        s_spec,
        pl.BlockSpec((None, LANES, GLA_DK), lambda i, h: (layer, 0, h)),
        pl.BlockSpec((None, 1, GLA_DK), lambda i, h: (layer, 0, h)),
        _layer_spec(layer, 1, GLA_DV),
    ]
    args = [proj, proj, proj, proj, proj, state_gla, wgu_all, gb_all, nw_all]
    aliases = {}
    if new_state is not None:
        in_specs.append(pl.BlockSpec(memory_space=pl.ANY))
        args.append(new_state)
        aliases = {len(args) - 1: 1}
    return pl.pallas_call(
        functools.partial(_gla_sample_kernel, tb=tb),
        grid=(bs // tb, GLA_HEADS),
        in_specs=in_specs,
        out_specs=[pl.BlockSpec((tb, GLA_DV), lambda i, h: (i, h)), s_spec],
        out_shape=[
            jax.ShapeDtypeStruct((bs, GLA_VAL_DIM), F32),
            jax.ShapeDtypeStruct(state_gla.shape, F32),
        ],
        input_output_aliases=aliases,
        compiler_params=_params("parallel", "parallel"),
        name="gla_rec_sample",
    )(*args)


def _out_ln_kernel(o_ref, w_ref, x_ref, g_ref, b_ref, y_ref):
    tm = x_ref.shape[0]
    rb = min(OUT_ROWS, tm)
    for r in range(tm // rb):
        rows = slice(r * rb, (r + 1) * rb)
        y = jnp.dot(o_ref[rows, :].astype(BF16), w_ref[...], preferred_element_type=F32)
        y_ref[rows, :] = _layer_norm(DN_ALPHA * x_ref[rows, :] + y, g_ref[...], b_ref[...])


def _out_ln(o2d, x2d, layer, mix_layer, w_out_all, g_all, b_all):
    m = x2d.shape[0]
    tm = min(512, m)
    return pl.pallas_call(
        _out_ln_kernel,
        grid=(m // tm,),
        in_specs=[
            pl.BlockSpec((tm, GLA_VAL_DIM), lambda i: (i, 0)),
            _layer_spec(layer, GLA_VAL_DIM, D_MODEL),
            pl.BlockSpec((tm, D_MODEL), lambda i: (i, 0)),
            _layer_spec(mix_layer, 1, D_MODEL), _layer_spec(mix_layer, 1, D_MODEL),
        ],
        out_specs=pl.BlockSpec((tm, D_MODEL), lambda i: (i, 0)),
        out_shape=jax.ShapeDtypeStruct(x2d.shape, F32),
        compiler_params=_params("parallel"),
        name="gla_out_ln",
    )(o2d, w_out_all, x2d, g_all, b_all)


def kernel(x_prompt, x_sample, state_pool, state_gla, pool_w, pool_scale, gla_w_in, gla_w_gate_up,
           gla_gate_bias, gla_norm_w, gla_w_out, ln_mix_g, ln_mix_b, mlp_w1, mlp_b1, mlp_w2, mlp_b2,
           ln_ffn_g, ln_ffn_b):
    batch, seq, _ = x_prompt.shape
    bs = x_sample.shape[0]
    xp = x_prompt.reshape(batch * seq, D_MODEL)
    xs = x_sample.reshape(bs, D_MODEL)

    vec = lambda a: a.reshape(a.shape[0], 1, a.shape[1])
    w_out = gla_w_out.astype(BF16)
    mlp_w = {0: (mlp_w1[0].astype(BF16), mlp_w2[0].astype(BF16))}
    w_in = jnp.pad(gla_w_in.astype(BF16), ((0, 0), (0, 0), (0, PROJ_PAD - gla_w_in.shape[-1])))
    wgu = jnp.pad(gla_w_gate_up.astype(BF16), ((0, 0), (0, LANES - GLA_GATE_RANK), (0, 0)))
    pool_scale, gate_bias, norm_w, mix_g, mix_b, b1, b2, ffn_g, ffn_b = map(
        vec, (pool_scale, gla_gate_bias, gla_norm_w, ln_mix_g, ln_mix_b, mlp_b1, mlp_b2, ln_ffn_g, ln_ffn_b))

    state_rows = jnp.transpose(state_pool, (0, 2, 1, 3))
    pool_in_p, gla_states_p = [], []
    new_rows = None
    new_gla_sample = None
    for i in range(DEPTH):
        j = i // 2
        if i % 2 == 0:
            pool_in_p.append(xp.reshape(batch, seq, D_MODEL)[:, seq - POOL_CTX:])
            xp = _pool_ln_prompt(xp, batch, seq, j, i, pool_w, pool_scale, mix_g, mix_b)
            xs, new_rows = _pool_ln_sample(xs, state_rows, new_rows, j, i, pool_w, pool_scale, mix_g, mix_b)
        else:
            ahead = (mlp_w1, mlp_w2, i + 1) if i + 1 < DEPTH else None
            proj_p, cast = _gla_proj(xp, j, w_in, ahead)
            if ahead:
                mlp_w[i + 1] = cast
            o_p, s_p, mlp_w[i] = _gla_rec_prompt(proj_p, batch, seq, j, wgu, gate_bias, norm_w, (mlp_w1, mlp_w2, i))
            gla_states_p.append(s_p)
            o_s, new_gla_sample = _gla_rec_sample(_gla_proj(xs, j, w_in)[0], state_gla, new_gla_sample, j, wgu,
                                                  gate_bias, norm_w)
            xp = _out_ln(o_p, xp, j, i, w_out, mix_g, mix_b)
            xs = _out_ln(o_s, xs, j, i, w_out, mix_g, mix_b)
        w1, w2 = mlp_w[i]
        xp, xs = _mlp_ln(xp, xs, i, w1, b1, w2, b2, ffn_g, ffn_b)

    new_pool_sample = jnp.transpose(new_rows, (0, 2, 1, 3))
    return (xp.reshape(batch, seq, D_MODEL), xs.reshape(bs, 1, D_MODEL), jnp.stack(pool_in_p),
            jnp.stack(gla_states_p), new_pool_sample, new_gla_sample)
```

```python
import functools

import jax
import jax.numpy as jnp
from jax import lax
from jax.experimental import pallas as pl
from jax.experimental.pallas import tpu as pltpu

F32 = jnp.float32
BF16 = jnp.bfloat16

D_MODEL = 2048
DEPTH = 4
PAST_LEN = 16384
POOL_WINDOWS = (2, 4, 8, 16)
POOL_GROUP = D_MODEL // len(POOL_WINDOWS)
POOL_CTX = max(POOL_WINDOWS) - 1
GLA_HEADS = 4
GLA_KEY_DIM = D_MODEL // 2
GLA_VAL_DIM = D_MODEL
GLA_DK = GLA_KEY_DIM // GLA_HEADS
GLA_DV = GLA_VAL_DIM // GLA_HEADS
GLA_GATE_RANK = 16
GLA_GATE_TEMP = 16.0
GLA_CHUNK = 16
D_FF = 4 * D_MODEL
LN_EPS = 1e-5
RMS_EPS = 1e-5
DN_ALPHA = (2 * DEPTH) ** 0.25
LOG2_E = 1.4426950408889634

LANES = 128
SUBLANES = 8
MXU_DIM = 256
VMEM_LIMIT_BYTES = 60 * 1024 * 1024

PROJ_QKVG = 2 * GLA_KEY_DIM + 2 * GLA_VAL_DIM
PROJ_TILE = 5 * MXU_DIM
PROJ_PAD = -(-(PROJ_QKVG + LANES) // PROJ_TILE) * PROJ_TILE
GLA_BLOCK = 128
CHUNKS_PER_BLOCK = GLA_BLOCK // GLA_CHUNK
POOL_ROWS = 256
MLP_ROWS = 256
OUT_ROWS = 128


def _params(*sem):
    return pltpu.CompilerParams(dimension_semantics=sem, vmem_limit_bytes=VMEM_LIMIT_BYTES)


def _layer_spec(layer, *block):
    return pl.BlockSpec((None,) + block, lambda *_: (layer,) + (0,) * len(block))


def _layer_norm(v, g, b):
    mu = jnp.mean(v, axis=-1, keepdims=True)
    c = v - mu
    var = jnp.mean(c * c, axis=-1, keepdims=True)
    return c * lax.rsqrt(var + LN_EPS) * g + b


def _split_bf16(z, parts):
    out = []
    for _ in range(parts - 1):
        hi = z.astype(BF16)
        out.append(hi)
        z = z - hi.astype(F32)
    out.append(z.astype(BF16))
    return out


def _mask_dot(mask_bf16, z, parts):
    terms = _split_bf16(z, parts)
    if mask_bf16.shape[1] % LANES == 0:
        return jnp.dot(jnp.concatenate([mask_bf16] * parts, axis=1), jnp.concatenate(terms, axis=0),
                       preferred_element_type=F32)
    return sum(jnp.dot(mask_bf16, t, preferred_element_type=F32) for t in terms)


def _log_sigmoid(x):
    return jnp.minimum(x, 0.0) - jnp.log(1.0 + jnp.exp(-jnp.abs(x)))


def _dot_nt(a, b):
    return lax.dot_general(a, b, (((1,), (1,)), ((), ())), preferred_element_type=F32)


def _dot_tn(a, b):
    return lax.dot_general(a, b, (((0,), (0,)), ((), ())), preferred_element_type=F32)


def _pool_ln_kernel(x_ref, halo_ref, bm_ref, bh_ref, w_ref, scale_ref, g_ref, b_ref, o_ref, *, tile_rows):
    j = pl.program_id(1)
    rb = min(POOL_ROWS, tile_rows)
    row = lax.broadcasted_iota(jnp.int32, (rb, 1), 0)
    for g, w in enumerate(POOL_WINDOWS):
        cols = slice(g * POOL_GROUP, (g + 1) * POOL_GROUP)
        for r in range(tile_rows // rb):
            xb = x_ref[r * rb:(r + 1) * rb, cols]
            if r == 0:
                halo = jnp.where(j == 0, 0.0, halo_ref[:, cols])
            else:
                halo = x_ref[r * rb - 16:r * rb, cols]
            win = _mask_dot(bm_ref[g], xb, 2)
            head = win[:16] + _mask_dot(bh_ref[g], halo, 2)
            win = jnp.concatenate([head, win[16:]], axis=0)
            count = jnp.minimum(j * tile_rows + r * rb + row + 1, w).astype(F32)
            p = win / count - xb
            y = jnp.dot(p.astype(BF16), w_ref[g].astype(BF16), preferred_element_type=F32)
            o_ref[r * rb:(r + 1) * rb, cols] = y * scale_ref[:, cols]
    ln_rows = min(128, tile_rows)
    for r in range(tile_rows // ln_rows):
        rows = slice(r * ln_rows, (r + 1) * ln_rows)
        v = DN_ALPHA * x_ref[rows, :] + o_ref[rows, :]
        o_ref[rows, :] = _layer_norm(v, g_ref[...], b_ref[...])


def _band_matrices(rb):
    i = jnp.arange(rb)[:, None]
    jj = jnp.arange(rb)[None, :]
    bm = jnp.stack([((i - jj >= 0) & (i - jj < w)) for w in POOL_WINDOWS]).astype(BF16)
    i16 = jnp.arange(16)[:, None]
    j16 = jnp.arange(16)[None, :]
    bh = jnp.stack([(j16 >= i16 + 17 - w) for w in POOL_WINDOWS]).astype(BF16)
    return bm, bh


def _pool_ln_prompt(x2d, batch, seq, layer, mix_layer, w_all, scale_all, g_all, b_all):
    tile_rows = min(1024, seq)
    nt = seq // tile_rows
    bm, bh = _band_matrices(min(POOL_ROWS, tile_rows))
    const = lambda *shape: pl.BlockSpec(shape, lambda bi, j: (0,) * len(shape))
    return pl.pallas_call(
        functools.partial(_pool_ln_kernel, tile_rows=tile_rows),
        grid=(batch, nt),
        in_specs=[
            pl.BlockSpec((tile_rows, D_MODEL), lambda bi, j: (bi * nt + j, 0)),
            pl.BlockSpec((16, D_MODEL), lambda bi, j: (jnp.maximum((bi * nt + j) * (tile_rows // 16) - 1, 0), 0)),
            const(*bm.shape), const(*bh.shape),
            _layer_spec(layer, *w_all.shape[1:]),
            _layer_spec(layer, 1, D_MODEL), _layer_spec(mix_layer, 1, D_MODEL), _layer_spec(mix_layer, 1, D_MODEL),
        ],
        out_specs=pl.BlockSpec((tile_rows, D_MODEL), lambda bi, j: (bi * nt + j, 0)),
        out_shape=jax.ShapeDtypeStruct(x2d.shape, F32),
        compiler_params=_params("parallel", "arbitrary"),
        name="pool_ln_prompt",
    )(x2d, x2d, bm, bh, w_all, scale_all, g_all, b_all)


def _pool_ln_sample_kernel(x_ref, st_ref, w_ref, scale_ref, g_ref, b_ref, *rest):
    o_ref, new_ref = rest[-2:]
    x = x_ref[...]
    new_ref[:POOL_CTX - 1] = st_ref[1:]
    new_ref[POOL_CTX - 1] = x
    for g, w in enumerate(POOL_WINDOWS):
        cols = slice(g * POOL_GROUP, (g + 1) * POOL_GROUP)
        xg = x[:, cols]
        win = xg
        for r in range(POOL_CTX - (w - 1), POOL_CTX):
            win = win + st_ref[r, :, cols]
        p = win / float(min(PAST_LEN + 1, w)) - xg
        y = jnp.dot(p.astype(BF16), w_ref[g].astype(BF16), preferred_element_type=F32)
        o_ref[:, cols] = y * scale_ref[:, cols]
    o_ref[...] = _layer_norm(DN_ALPHA * x + o_ref[...], g_ref[...], b_ref[...])


def _pool_ln_sample(xs, state_rows, new_rows, layer, mix_layer, w_all, scale_all, g_all, b_all):
    bs = xs.shape[0]
    tb = min(32, bs)
    st_spec = pl.BlockSpec((None, POOL_CTX, tb, D_MODEL), lambda i: (layer, 0, i, 0))
    in_specs = [
        pl.BlockSpec((tb, D_MODEL), lambda i: (i, 0)),
        st_spec,
        _layer_spec(layer, *w_all.shape[1:]),
        _layer_spec(layer, 1, D_MODEL), _layer_spec(mix_layer, 1, D_MODEL), _layer_spec(mix_layer, 1, D_MODEL),
    ]
    args = [xs, state_rows, w_all, scale_all, g_all, b_all]
    aliases = {}
    if new_rows is not None:
        in_specs.append(pl.BlockSpec(memory_space=pl.ANY))
        args.append(new_rows)
        aliases = {len(args) - 1: 1}
    return pl.pallas_call(
        _pool_ln_sample_kernel,
        grid=(bs // tb,),
        in_specs=in_specs,
        out_specs=[pl.BlockSpec((tb, D_MODEL), lambda i: (i, 0)), st_spec],
        out_shape=[jax.ShapeDtypeStruct(xs.shape, F32), jax.ShapeDtypeStruct(state_rows.shape, F32)],
        input_output_aliases=aliases,
        compiler_params=_params("parallel"),
        name="pool_ln_sample",
    )(*args)


def _mlp_ln_kernel(x_ref, *rest, sample_rows):
    if sample_rows:
        xs_ref, w1_ref, b1_ref, w2_ref, b2_ref, g_ref, b_ref, o_ref, os_ref, xb_ref = rest
    else:
        w1_ref, b1_ref, w2_ref, b2_ref, g_ref, b_ref, o_ref, xb_ref = rest
    f = pl.program_id(1)
    tm = x_ref.shape[0]
    total = tm + sample_rows

    @pl.when(f == 0)
    def _():
        xb_ref[:tm, :] = x_ref[...].astype(BF16)
        o_ref[...] = jnp.zeros_like(o_ref)
        if sample_rows:
            xb_ref[tm:, :] = xs_ref[...].astype(BF16)
            os_ref[...] = jnp.zeros_like(os_ref)

    chunks = max(tm // MLP_ROWS, 1) + (1 if sample_rows else 0)
    rb = total // chunks

    def accumulate(finish):
        for c in range(chunks):
            lo, hi = c * rb, (c + 1) * rb
            h = jnp.dot(xb_ref[lo:hi, :], w1_ref[...], preferred_element_type=F32) + b1_ref[...]
            h = jnp.square(jnp.maximum(h, 0.0))
            part = jnp.dot(h.astype(BF16), w2_ref[...], preferred_element_type=F32)
            pieces = []
            if lo < tm:
                pieces.append((x_ref, o_ref, lo, min(hi, tm), 0))
            if hi > tm:
                pieces.append((xs_ref, os_ref, max(lo, tm) - tm, hi - tm, max(lo, tm) - lo))
            for src, dst, a, b, off in pieces:
                acc = dst[a:b, :] + part[off:off + b - a]
                if finish:
                    v = DN_ALPHA * src[a:b, :] + (acc + b2_ref[...])
                    acc = _layer_norm(v, g_ref[...], b_ref[...])
                dst[a:b, :] = acc

    last = pl.num_programs(1) - 1
    pl.when(f < last)(functools.partial(accumulate, False))
    pl.when(f == last)(functools.partial(accumulate, True))


def _mlp_ln(x2d, xs2d, layer, w1, b1_all, w2, b2_all, g_all, b_all):
    m, bs = x2d.shape[0], xs2d.shape[0]
    tm = min(1024, m)
    tiles = m // tm
    ts = bs // tiles
    chunks = max(tm // MLP_ROWS, 1) + 1
    if bs % tiles or ts % 16 or (tm + ts) % chunks or ((tm + ts) // chunks) % 16:
        return tuple(_mlp_ln_call(a, None, 0, layer, w1, b1_all, w2, b2_all, g_all, b_all)[0] for a in (x2d, xs2d))
    return _mlp_ln_call(x2d, xs2d, ts, layer, w1, b1_all, w2, b2_all, g_all, b_all)


def _mlp_ln_call(x2d, xs2d, ts, layer, w1, b1_all, w2, b2_all, g_all, b_all):
    m = x2d.shape[0]
    tm = min(1024, m)
    tf = 1024
    vec = lambda: pl.BlockSpec((None, 1, D_MODEL), lambda i, f: (layer, 0, 0))
    rows = lambda n: pl.BlockSpec((n, D_MODEL), lambda i, f: (i, 0))
    sample = [xs2d] if ts else []
    return pl.pallas_call(
        functools.partial(_mlp_ln_kernel, sample_rows=ts),
        grid=(m // tm, D_FF // tf),
        in_specs=[rows(tm)] + [rows(ts)] * bool(ts) + [
            pl.BlockSpec((D_MODEL, tf), lambda i, f: (0, f)),
            pl.BlockSpec((None, 1, tf), lambda i, f: (layer, 0, f)),
            pl.BlockSpec((tf, D_MODEL), lambda i, f: (f, 0)),
            vec(), vec(), vec(),
        ],
        out_specs=[rows(tm)] + [rows(ts)] * bool(ts),
        out_shape=[jax.ShapeDtypeStruct(a.shape, F32) for a in [x2d] + sample],
        scratch_shapes=[pltpu.VMEM((tm + ts, D_MODEL), BF16)],
        compiler_params=_params("parallel", "arbitrary"),
        name="mlp_ln",
    )(x2d, *sample, w1, b1_all, w2, b2_all, g_all, b_all)


class _CastJob:
    def __init__(self, sources, steps, step_of):
        nb = 1 << (steps.bit_length() - 1)
        slab = lambda *idx: jnp.minimum(step_of(*idx), nb - 1)
        self.args, self.in_specs, self.out_specs, self.out_shape = [], [], [], []
        for w_all, layer in sources:
            _, rows, cols = w_all.shape
            self.args.append(w_all)
            self.in_specs.append(pl.BlockSpec((None, rows // nb, cols), lambda *idx, l=layer: (l, slab(*idx), 0)))
            self.out_specs.append(pl.BlockSpec((rows // nb, cols), lambda *idx: (slab(*idx), 0)))
            self.out_shape.append(jax.ShapeDtypeStruct((rows, cols), BF16))


def _proj_kernel(x_ref, w_ref, o_ref, xb_ref):
    @pl.when(pl.program_id(1) == 0)
    def _():
        xb_ref[...] = x_ref[...].astype(BF16)

    o_ref[...] = jnp.dot(xb_ref[...], w_ref[...], preferred_element_type=F32)


def _gla_proj(x2d, layer, w_in_all):
    m = x2d.shape[0]
    tm = min(1024, m)
    return pl.pallas_call(
        _proj_kernel,
        grid=(m // tm, PROJ_PAD // PROJ_TILE),
        in_specs=[
            pl.BlockSpec((tm, D_MODEL), lambda i, n: (i, 0)),
            pl.BlockSpec((None, D_MODEL, PROJ_TILE), lambda i, n: (layer, 0, n)),
        ],
        out_specs=pl.BlockSpec((tm, PROJ_TILE), lambda i, n: (i, n)),
        out_shape=jax.ShapeDtypeStruct((m, PROJ_PAD), F32),
        scratch_shapes=[pltpu.VMEM((tm, D_MODEL), BF16)],
        compiler_params=_params("parallel", "arbitrary"),
        name="gla_proj",
    )(x2d, w_in_all)


def _rms_gate(o, og, nw):
    o = o * lax.rsqrt(jnp.mean(o * o, axis=-1, keepdims=True) + RMS_EPS) * nw
    return o * (og * jax.nn.sigmoid(og))


def _gla_block(r0, q_ref, k_ref, v_ref, og_ref, nw_ref, o_ref, s_ref, bc_ref, a_ref):
    scale = GLA_DK ** -0.5
    half = GLA_CHUNK // 2
    chunks = range(CHUNKS_PER_BLOCK)
    blk = slice(r0, r0 + GLA_BLOCK)
    a_blk = a_ref.at[blk]

    tot = [bc_ref[r0 + (c + 1) * GLA_CHUNK - 1:r0 + (c + 1) * GLA_CHUNK, :] for c in chunks]
    before = [jnp.zeros_like(tot[0])]
    for c in chunks[1:]:
        before.append(before[-1] + tot[c - 1])
    after = [jnp.zeros_like(tot[0])]
    for c in reversed(chunks[:-1]):
        after.insert(0, after[0] + tot[c + 1])
    total = before[-1] + tot[-1]

    v = v_ref[blk, :].astype(BF16)
    qs, ks, bcs, qd, kd, q_hi, k_lo = [], [], [], [], [], [], []
    for c in chunks:
        rows = slice(r0 + c * GLA_CHUNK, r0 + (c + 1) * GLA_CHUNK)
        bcc = bc_ref[rows, :]
        qc = q_ref[rows, :] * scale
        kc = k_ref[rows, :]
        mid = bcc[half:half + 1]
        qs.append(qc)
        ks.append(kc)
        bcs.append(bcc)
        qd.append(qc * jnp.exp2(bcc))
        kd.append(kc * jnp.exp2(tot[c] - bcc))
        q_hi.append(qc[half:] * jnp.exp2(bcc[half:] - mid))
        k_lo.append(kc[:half] * jnp.exp2(mid - bcc[:half]))

    s0 = s_ref[...]
    q_in = jnp.concatenate([qd[c] * jnp.exp2(before[c]) for c in chunks], axis=0)
    o = jnp.dot(q_in.astype(BF16), s0.astype(BF16), preferred_element_type=F32)

    kd_all = jnp.concatenate(kd, axis=0).astype(BF16)
    lhs = []
    for jc in chunks[:-1]:
        lhs.append(qd[jc + 1])
        lhs.extend(qd[c] * jnp.exp2(before[c] - before[jc + 1]) for c in chunks[jc + 2:])
    pair = _dot_nt(jnp.concatenate(lhs, axis=0).astype(BF16), kd_all)
    a_blk[...] = jnp.zeros((GLA_BLOCK, GLA_BLOCK), F32)
    off = 0
    for jc in chunks[:-1]:
        lo = (jc + 1) * GLA_CHUNK
        n = GLA_BLOCK - lo
        lane = lax.broadcasted_iota(jnp.int32, (n, GLA_BLOCK), 1)
        keep = (lane >= jc * GLA_CHUNK) & (lane < lo)
        a_blk[lo:, :] += jnp.where(keep, pair[off:off + n], 0.0)
        off += n

    zeros = jnp.zeros((half, GLA_DK), F32)
    k_lo_all = jnp.concatenate([t for c in chunks for t in (k_lo[c], zeros)], axis=0).astype(BF16)
    cross = _dot_nt(jnp.concatenate(q_hi, axis=0).astype(BF16), k_lo_all)
    row = lax.broadcasted_iota(jnp.int32, (half, GLA_DK), 0)
    lane = lax.broadcasted_iota(jnp.int32, (half, GLA_BLOCK), 1)
    for c in chunks:
        for hs in range(2):
            rs = slice(hs * half, (hs + 1) * half)
            qh, kh, bh = qs[c][rs], ks[c][rs], bcs[c][rs]
            col0 = c * GLA_CHUNK + hs * half
            acc = jnp.zeros((half, GLA_BLOCK), F32)
            if hs == 1:
                in_chunk = (lane >= c * GLA_CHUNK) & (lane < col0)
                acc = jnp.where(in_chunk, cross[c * half:(c + 1) * half], 0.0)
            for j in range(half):
                d = bh - bh[j:j + 1]
                e = jnp.exp2(jnp.where(row >= j, d, -jnp.inf)) if j else jnp.exp2(d)
                t = qh * e * kh[j:j + 1]
                acc = acc + jnp.where(lane == col0 + j, jnp.sum(t, axis=1, keepdims=True), 0.0)
            a_blk[col0:col0 + half, :] += acc

    o = o + jnp.dot(a_blk[...].astype(BF16), v, preferred_element_type=F32)
    o_ref[blk, :] = _rms_gate(o, og_ref[blk, :], nw_ref[...]).astype(o_ref.dtype)

    k_out = jnp.concatenate([kd[c] * jnp.exp2(after[c]) for c in chunks], axis=0)
    upd = _dot_tn(k_out.astype(BF16), v)
    gcol = jnp.transpose(jnp.broadcast_to(jnp.exp2(total), (LANES, GLA_DK)))
    for n in range(GLA_DV // LANES):
        cols = slice(n * LANES, (n + 1) * LANES)
        s_ref[:, cols] = s0[:, cols] * gcol + upd[:, cols]


def _gla_rec_kernel(q_ref, k_ref, v_ref, og_ref, gl_ref, wgu_ref, gb_ref, nw_ref, tril_ref, *rest, cast):
    srcs, (o_ref, sout_ref), dsts = rest[:cast], rest[cast:cast + 2], rest[cast + 2:2 * cast + 2]
    s_ref, bc_ref, a_ref = rest[2 * cast + 2:]
    for src_ref, dst_ref in zip(srcs, dsts):
        dst_ref[...] = src_ref[...].astype(BF16)
    l = pl.program_id(2)

    @pl.when(l == 0)
    def _():
        s_ref[...] = jnp.zeros_like(s_ref)

    gk = jnp.dot(gl_ref[...].astype(BF16), wgu_ref[...], preferred_element_type=F32) + gb_ref[...]
    la = _log_sigmoid(gk) * (LOG2_E / GLA_GATE_TEMP)
    blocks = q_ref.shape[0] // GLA_BLOCK
    for i in range(blocks):
        blk = slice(i * GLA_BLOCK, (i + 1) * GLA_BLOCK)
        bc_ref[blk, :] = _mask_dot(tril_ref[...], la[blk], 2)
    for i in range(blocks):
        _gla_block(i * GLA_BLOCK, q_ref, k_ref, v_ref, og_ref, nw_ref, o_ref, s_ref, bc_ref, a_ref)

    @pl.when(l == pl.num_programs(2) - 1)
    def _():
        sout_ref[...] = s_ref[...]


def _chunk_tril():
    i = jnp.arange(GLA_BLOCK)[:, None]
    j = jnp.arange(GLA_BLOCK)[None, :]
    return ((i // GLA_CHUNK == j // GLA_CHUNK) & (j <= i)).astype(BF16)


def _gla_rec_prompt(proj, batch, seq, layer, wgu_all, gb_all, nw_all, cast_sources=()):
    tl = min(1024, seq)
    nl = seq // tl
    kq = GLA_KEY_DIM // GLA_DK
    kv = 2 * GLA_KEY_DIM // GLA_DV
    row = lambda b, h, l: b * nl + l
    job = None
    if cast_sources:
        job = _CastJob(cast_sources, steps=batch * GLA_HEADS * nl,
                       step_of=lambda b, h, l: (b * GLA_HEADS + h) * nl + l)
    outs = pl.pallas_call(
        functools.partial(_gla_rec_kernel, cast=len(cast_sources)),
        grid=(batch, GLA_HEADS, nl),
        in_specs=[
            pl.BlockSpec((tl, GLA_DK), lambda b, h, l: (row(b, h, l), h)),
            pl.BlockSpec((tl, GLA_DK), lambda b, h, l: (row(b, h, l), kq + h)),
            pl.BlockSpec((tl, GLA_DV), lambda b, h, l: (row(b, h, l), kv + h)),
            pl.BlockSpec((tl, GLA_DV), lambda b, h, l: (row(b, h, l), kv + GLA_HEADS + h)),
            pl.BlockSpec((tl, LANES), lambda b, h, l: (row(b, h, l), PROJ_QKVG // LANES)),
            pl.BlockSpec((None, LANES, GLA_DK), lambda b, h, l: (layer, 0, h)),
            pl.BlockSpec((None, 1, GLA_DK), lambda b, h, l: (layer, 0, h)),
            _layer_spec(layer, 1, GLA_DV),
            pl.BlockSpec((GLA_BLOCK, GLA_BLOCK), lambda b, h, l: (0, 0)),
        ] + (job.in_specs if job else []),
        out_specs=[
            pl.BlockSpec((tl, GLA_DV), lambda b, h, l: (row(b, h, l), h)),
            pl.BlockSpec((None, None, GLA_DK, GLA_DV), lambda b, h, l: (b, h, 0, 0)),
        ] + (job.out_specs if job else []),
        out_shape=[
            jax.ShapeDtypeStruct((batch * seq, GLA_VAL_DIM), BF16),
            jax.ShapeDtypeStruct((batch, GLA_HEADS, GLA_DK, GLA_DV), F32),
        ] + (job.out_shape if job else []),
        scratch_shapes=[
            pltpu.VMEM((GLA_DK, GLA_DV), F32),
            pltpu.VMEM((tl, GLA_DK), F32),
            pltpu.VMEM((tl, GLA_BLOCK), F32),
        ],
        compiler_params=_params("arbitrary", "arbitrary", "arbitrary"),
        name="gla_rec_prompt",
    )(proj, proj, proj, proj, proj, wgu_all, gb_all, nw_all, _chunk_tril(), *(job.args if job else []))
    return outs[0], outs[1], tuple(outs[2:])


def _gla_sample_kernel(q_ref, k_ref, v_ref, og_ref, gl_ref, s_ref, wgu_ref, gb_ref, nw_ref, *rest, tb):
    o_ref, sout_ref = rest[-2:]
    gk = jnp.dot(gl_ref[...].astype(BF16), wgu_ref[...], preferred_element_type=F32) + gb_ref[...]
    a = jnp.exp(_log_sigmoid(gk) / GLA_GATE_TEMP)
    q = q_ref[...] * (GLA_DK ** -0.5)
    k = k_ref[...]
    v = v_ref[...]
    qk = jnp.sum(q * k, axis=1, keepdims=True)
    pad = jnp.zeros((LANES - 3 * tb, GLA_DK), F32)
    cols = jnp.transpose(jnp.concatenate([a, k, q * a, pad], axis=0))
    outs = []
    for b in range(tb):
        s0 = s_ref[b]
        vb = v[b:b + 1]
        outs.append(qk[b:b + 1] * vb + jnp.sum(cols[:, 2 * tb + b:2 * tb + b + 1] * s0, axis=0, keepdims=True))
        sout_ref[b] = s0 * cols[:, b:b + 1] + cols[:, tb + b:tb + b + 1] * vb
    o_ref[...] = _rms_gate(jnp.concatenate(outs, axis=0), og_ref[...], nw_ref[...])


def _gla_rec_sample(proj, state_gla, new_state, layer, wgu_all, gb_all, nw_all):
    bs = proj.shape[0]
    tb = min(2 * SUBLANES, bs)
    kq = GLA_KEY_DIM // GLA_DK
    kv = 2 * GLA_KEY_DIM // GLA_DV
    s_spec = pl.BlockSpec((None, tb, None, GLA_DK, GLA_DV), lambda i, h: (layer, i, h, 0, 0))
    in_specs = [
        pl.BlockSpec((tb, GLA_DK), lambda i, h: (i, h)),
        pl.BlockSpec((tb, GLA_DK), lambda i, h: (i, kq + h)),
        pl.BlockSpec((tb, GLA_DV), lambda i, h: (i, kv + h)),
        pl.BlockSpec((tb, GLA_DV), lambda i, h: (i, kv + GLA_HEADS + h)),
        pl.BlockSpec((tb, LANES), lambda i, h: (i, PROJ_QKVG // LANES)),
        s_spec,
        pl.BlockSpec((None, LANES, GLA_DK), lambda i, h: (layer, 0, h)),
        pl.BlockSpec((None, 1, GLA_DK), lambda i, h: (layer, 0, h)),
        _layer_spec(layer, 1, GLA_DV),
    ]
    args = [proj, proj, proj, proj, proj, state_gla, wgu_all, gb_all, nw_all]
    aliases = {}
    if new_state is not None:
        in_specs.append(pl.BlockSpec(memory_space=pl.ANY))
        args.append(new_state)
        aliases = {len(args) - 1: 1}
    return pl.pallas_call(
        functools.partial(_gla_sample_kernel, tb=tb),
        grid=(bs // tb, GLA_HEADS),
        in_specs=in_specs,
        out_specs=[pl.BlockSpec((tb, GLA_DV), lambda i, h: (i, h)), s_spec],
        out_shape=[
            jax.ShapeDtypeStruct((bs, GLA_VAL_DIM), F32),
            jax.ShapeDtypeStruct(state_gla.shape, F32),
        ],
        input_output_aliases=aliases,
        compiler_params=_params("parallel", "parallel"),
        name="gla_rec_sample",
    )(*args)


def _out_ln_kernel(o_ref, w_ref, x_ref, g_ref, b_ref, y_ref):
    tm = x_ref.shape[0]
    rb = min(OUT_ROWS, tm)
    for r in range(tm // rb):
        rows = slice(r * rb, (r + 1) * rb)
        y = jnp.dot(o_ref[rows, :].astype(BF16), w_ref[...], preferred_element_type=F32)
        y_ref[rows, :] = _layer_norm(DN_ALPHA * x_ref[rows, :] + y, g_ref[...], b_ref[...])


def _out_ln(o2d, x2d, mix_layer, w_out, g_all, b_all):
    m = x2d.shape[0]
    tm = min(512, m)
    return pl.pallas_call(
        _out_ln_kernel,
        grid=(m // tm,),
        in_specs=[
            pl.BlockSpec((tm, GLA_VAL_DIM), lambda i: (i, 0)),
            pl.BlockSpec((GLA_VAL_DIM, D_MODEL), lambda i: (0, 0)),
            pl.BlockSpec((tm, D_MODEL), lambda i: (i, 0)),
            _layer_spec(mix_layer, 1, D_MODEL), _layer_spec(mix_layer, 1, D_MODEL),
        ],
        out_specs=pl.BlockSpec((tm, D_MODEL), lambda i: (i, 0)),
        out_shape=jax.ShapeDtypeStruct(x2d.shape, F32),
        compiler_params=_params("parallel"),
        name="gla_out_ln",
    )(o2d, w_out, x2d, g_all, b_all)


def kernel(x_prompt, x_sample, state_pool, state_gla, pool_w, pool_scale, gla_w_in, gla_w_gate_up,
           gla_gate_bias, gla_norm_w, gla_w_out, ln_mix_g, ln_mix_b, mlp_w1, mlp_b1, mlp_w2, mlp_b2,
           ln_ffn_g, ln_ffn_b):
    batch, seq, _ = x_prompt.shape
    bs = x_sample.shape[0]
    xp = x_prompt.reshape(batch * seq, D_MODEL)
    xs = x_sample.reshape(bs, D_MODEL)

    vec = lambda a: a.reshape(a.shape[0], 1, a.shape[1])
    mlp_w = {0: (mlp_w1[0].astype(BF16), mlp_w2[0].astype(BF16))}
    w_in = jnp.pad(gla_w_in.astype(BF16), ((0, 0), (0, 0), (0, PROJ_PAD - gla_w_in.shape[-1])))
    wgu = jnp.pad(gla_w_gate_up.astype(BF16), ((0, 0), (0, LANES - GLA_GATE_RANK), (0, 0)))
    pool_scale, gate_bias, norm_w, mix_g, mix_b, b1, b2, ffn_g, ffn_b = map(
        vec, (pool_scale, gla_gate_bias, gla_norm_w, ln_mix_g, ln_mix_b, mlp_b1, mlp_b2, ln_ffn_g, ln_ffn_b))

    state_rows = jnp.transpose(state_pool, (0, 2, 1, 3))
    pool_in_p, gla_states_p = [], []
    new_rows = None
    new_gla_sample = None
    for i in range(DEPTH):
        j = i // 2
        if i % 2 == 0:
            pool_in_p.append(xp.reshape(batch, seq, D_MODEL)[:, seq - POOL_CTX:])
            xp = _pool_ln_prompt(xp, batch, seq, j, i, pool_w, pool_scale, mix_g, mix_b)
            xs, new_rows = _pool_ln_sample(xs, state_rows, new_rows, j, i, pool_w, pool_scale, mix_g, mix_b)
        else:
            mlp_layers = [n for n in (i, i + 1) if n < DEPTH]
            sources = [(gla_w_out, j)] + [(w, n) for n in mlp_layers for w in (mlp_w1, mlp_w2)]
            o_p, s_p, cast = _gla_rec_prompt(_gla_proj(xp, j, w_in), batch, seq, j, wgu, gate_bias, norm_w, sources)
            w_out = cast[0]
            for n, pair in zip(mlp_layers, zip(cast[1::2], cast[2::2])):
                mlp_w[n] = pair
            gla_states_p.append(s_p)
            o_s, new_gla_sample = _gla_rec_sample(_gla_proj(xs, j, w_in), state_gla, new_gla_sample, j, wgu,
                                                  gate_bias, norm_w)
            xp = _out_ln(o_p, xp, i, w_out, mix_g, mix_b)
            xs = _out_ln(o_s, xs, i, w_out, mix_g, mix_b)
        w1, w2 = mlp_w[i]
        xp, xs = _mlp_ln(xp, xs, i, w1, b1, w2, b2, ffn_g, ffn_b)

    new_pool_sample = jnp.transpose(new_rows, (0, 2, 1, 3))
    return (xp.reshape(batch, seq, D_MODEL), xs.reshape(bs, 1, D_MODEL), jnp.stack(pool_in_p),
            jnp.stack(gla_states_p), new_pool_sample, new_gla_sample)
```

```python
import functools

import jax
import jax.numpy as jnp
from jax import lax
from jax.experimental import pallas as pl
from jax.experimental.pallas import tpu as pltpu

F32 = jnp.float32
BF16 = jnp.bfloat16

D_MODEL = 2048
DEPTH = 4
PAST_LEN = 16384
POOL_WINDOWS = (2, 4, 8, 16)
POOL_GROUP = D_MODEL // len(POOL_WINDOWS)
POOL_CTX = max(POOL_WINDOWS) - 1
GLA_HEADS = 4
GLA_KEY_DIM = D_MODEL // 2
GLA_VAL_DIM = D_MODEL
GLA_DK = GLA_KEY_DIM // GLA_HEADS
GLA_DV = GLA_VAL_DIM // GLA_HEADS
GLA_GATE_RANK = 16
GLA_GATE_TEMP = 16.0
GLA_CHUNK = 16
D_FF = 4 * D_MODEL
LN_EPS = 1e-5
RMS_EPS = 1e-5
DN_ALPHA = (2 * DEPTH) ** 0.25
LOG2_E = 1.4426950408889634

LANES = 128
SUBLANES = 8
MXU_DIM = 256
VMEM_LIMIT_BYTES = 60 * 1024 * 1024

PROJ_QKVG = 2 * GLA_KEY_DIM + 2 * GLA_VAL_DIM
PROJ_TILE = 5 * MXU_DIM
PROJ_PAD = -(-(PROJ_QKVG + LANES) // PROJ_TILE) * PROJ_TILE
GLA_BLOCK = 128
CHUNKS_PER_BLOCK = GLA_BLOCK // GLA_CHUNK
POOL_ROWS = 256
MLP_ROWS = 256
OUT_ROWS = 128


def _params(*sem):
    return pltpu.CompilerParams(dimension_semantics=sem, vmem_limit_bytes=VMEM_LIMIT_BYTES)


def _layer_spec(layer, *block):
    return pl.BlockSpec((None,) + block, lambda *_: (layer,) + (0,) * len(block))


def _layer_norm(v, g, b):
    mu = jnp.mean(v, axis=-1, keepdims=True)
    c = v - mu
    var = jnp.mean(c * c, axis=-1, keepdims=True)
    return c * lax.rsqrt(var + LN_EPS) * g + b


def _split_bf16(z, parts):
    out = []
    for _ in range(parts - 1):
        hi = z.astype(BF16)
        out.append(hi)
        z = z - hi.astype(F32)
    out.append(z.astype(BF16))
    return out


def _mask_dot(mask_bf16, z, parts):
    terms = _split_bf16(z, parts)
    if mask_bf16.shape[1] % LANES == 0:
        return jnp.dot(jnp.concatenate([mask_bf16] * parts, axis=1), jnp.concatenate(terms, axis=0),
                       preferred_element_type=F32)
    return sum(jnp.dot(mask_bf16, t, preferred_element_type=F32) for t in terms)


def _log_sigmoid(x):
    return jnp.minimum(x, 0.0) - jnp.log(1.0 + jnp.exp(-jnp.abs(x)))


def _dot_nt(a, b):
    return lax.dot_general(a, b, (((1,), (1,)), ((), ())), preferred_element_type=F32)


def _dot_tn(a, b):
    return lax.dot_general(a, b, (((0,), (0,)), ((), ())), preferred_element_type=F32)


def _pool_ln_kernel(x_ref, halo_ref, bm_ref, bh_ref, w_ref, scale_ref, g_ref, b_ref, *rest, tile_rows, cast):
    o_ref = rest[cast]
    for src_ref, dst_ref in zip(rest[:cast], rest[cast + 1:]):
        dst_ref[...] = src_ref[...].astype(BF16)
    j = pl.program_id(1)
    rb = min(POOL_ROWS, tile_rows)
    row = lax.broadcasted_iota(jnp.int32, (rb, 1), 0)
    for g, w in enumerate(POOL_WINDOWS):
        cols = slice(g * POOL_GROUP, (g + 1) * POOL_GROUP)
        for r in range(tile_rows // rb):
            xb = x_ref[r * rb:(r + 1) * rb, cols]
            if r == 0:
                halo = jnp.where(j == 0, 0.0, halo_ref[:, cols])
            else:
                halo = x_ref[r * rb - 16:r * rb, cols]
            win = _mask_dot(bm_ref[g], xb, 2)
            head = win[:16] + _mask_dot(bh_ref[g], halo, 2)
            win = jnp.concatenate([head, win[16:]], axis=0)
            count = jnp.minimum(j * tile_rows + r * rb + row + 1, w).astype(F32)
            p = win / count - xb
            y = jnp.dot(p.astype(BF16), w_ref[g].astype(BF16), preferred_element_type=F32)
            o_ref[r * rb:(r + 1) * rb, cols] = y * scale_ref[:, cols]
    ln_rows = min(128, tile_rows)
    for r in range(tile_rows // ln_rows):
        rows = slice(r * ln_rows, (r + 1) * ln_rows)
        v = DN_ALPHA * x_ref[rows, :] + o_ref[rows, :]
        o_ref[rows, :] = _layer_norm(v, g_ref[...], b_ref[...])


def _band_matrices(rb):
    i = jnp.arange(rb)[:, None]
    jj = jnp.arange(rb)[None, :]
    bm = jnp.stack([((i - jj >= 0) & (i - jj < w)) for w in POOL_WINDOWS]).astype(BF16)
    i16 = jnp.arange(16)[:, None]
    j16 = jnp.arange(16)[None, :]
    bh = jnp.stack([(j16 >= i16 + 17 - w) for w in POOL_WINDOWS]).astype(BF16)
    return bm, bh


def _pool_ln_prompt(x2d, batch, seq, layer, mix_layer, w_all, scale_all, g_all, b_all, cast_sources=()):
    tile_rows = min(256 if cast_sources else 1024, seq)
    nt = seq // tile_rows
    bm, bh = _band_matrices(min(POOL_ROWS, tile_rows))
    const = lambda *shape: pl.BlockSpec(shape, lambda bi, j: (0,) * len(shape))
    job = _CastJob(cast_sources, steps=batch * nt, step_of=lambda bi, j: bi * nt + j) if cast_sources else None
    outs = pl.pallas_call(
        functools.partial(_pool_ln_kernel, tile_rows=tile_rows, cast=len(cast_sources)),
        grid=(batch, nt),
        in_specs=[
            pl.BlockSpec((tile_rows, D_MODEL), lambda bi, j: (bi * nt + j, 0)),
            pl.BlockSpec((16, D_MODEL), lambda bi, j: (jnp.maximum((bi * nt + j) * (tile_rows // 16) - 1, 0), 0)),
            const(*bm.shape), const(*bh.shape),
            _layer_spec(layer, *w_all.shape[1:]),
            _layer_spec(layer, 1, D_MODEL), _layer_spec(mix_layer, 1, D_MODEL), _layer_spec(mix_layer, 1, D_MODEL),
        ] + (job.in_specs if job else []),
        out_specs=[pl.BlockSpec((tile_rows, D_MODEL), lambda bi, j: (bi * nt + j, 0))]
        + (job.out_specs if job else []),
        out_shape=[jax.ShapeDtypeStruct(x2d.shape, F32)] + (job.out_shape if job else []),
        compiler_params=_params("arbitrary", "arbitrary"),
        name="pool_ln_prompt",
    )(x2d, x2d, bm, bh, w_all, scale_all, g_all, b_all, *(job.args if job else []))
    return outs[0], tuple(outs[1:])


def _pool_ln_sample_kernel(x_ref, st_ref, w_ref, scale_ref, g_ref, b_ref, *rest):
    o_ref, new_ref = rest[-2:]
    x = x_ref[...]
    new_ref[:POOL_CTX - 1] = st_ref[1:]
    new_ref[POOL_CTX - 1] = x
    for g, w in enumerate(POOL_WINDOWS):
        cols = slice(g * POOL_GROUP, (g + 1) * POOL_GROUP)
        xg = x[:, cols]
        win = xg
        for r in range(POOL_CTX - (w - 1), POOL_CTX):
            win = win + st_ref[r, :, cols]
        p = win / float(min(PAST_LEN + 1, w)) - xg
        y = jnp.dot(p.astype(BF16), w_ref[g].astype(BF16), preferred_element_type=F32)
        o_ref[:, cols] = y * scale_ref[:, cols]
    o_ref[...] = _layer_norm(DN_ALPHA * x + o_ref[...], g_ref[...], b_ref[...])


def _pool_ln_sample(xs, state_rows, new_rows, layer, mix_layer, w_all, scale_all, g_all, b_all):
    bs = xs.shape[0]
    tb = min(32, bs)
    st_spec = pl.BlockSpec((None, POOL_CTX, tb, D_MODEL), lambda i: (layer, 0, i, 0))
    in_specs = [
        pl.BlockSpec((tb, D_MODEL), lambda i: (i, 0)),
        st_spec,
        _layer_spec(layer, *w_all.shape[1:]),
        _layer_spec(layer, 1, D_MODEL), _layer_spec(mix_layer, 1, D_MODEL), _layer_spec(mix_layer, 1, D_MODEL),
    ]
    args = [xs, state_rows, w_all, scale_all, g_all, b_all]
    aliases = {}
    if new_rows is not None:
        in_specs.append(pl.BlockSpec(memory_space=pl.ANY))
        args.append(new_rows)
        aliases = {len(args) - 1: 1}
    return pl.pallas_call(
        _pool_ln_sample_kernel,
        grid=(bs // tb,),
        in_specs=in_specs,
        out_specs=[pl.BlockSpec((tb, D_MODEL), lambda i: (i, 0)), st_spec],
        out_shape=[jax.ShapeDtypeStruct(xs.shape, F32), jax.ShapeDtypeStruct(state_rows.shape, F32)],
        input_output_aliases=aliases,
        compiler_params=_params("parallel"),
        name="pool_ln_sample",
    )(*args)


def _mlp_ln_kernel(x_ref, *rest, sample_rows):
    if sample_rows:
        xs_ref, w1_ref, b1_ref, w2_ref, b2_ref, g_ref, b_ref, o_ref, os_ref, xb_ref = rest
    else:
        w1_ref, b1_ref, w2_ref, b2_ref, g_ref, b_ref, o_ref, xb_ref = rest
    f = pl.program_id(1)
    tm = x_ref.shape[0]
    total = tm + sample_rows

    @pl.when(f == 0)
    def _():
        xb_ref[:tm, :] = x_ref[...].astype(BF16)
        o_ref[...] = jnp.zeros_like(o_ref)
        if sample_rows:
            xb_ref[tm:, :] = xs_ref[...].astype(BF16)
            os_ref[...] = jnp.zeros_like(os_ref)

    chunks = max(tm // MLP_ROWS, 1) + (1 if sample_rows else 0)
    rb = total // chunks

    def accumulate(finish):
        for c in range(chunks):
            lo, hi = c * rb, (c + 1) * rb
            h = jnp.dot(xb_ref[lo:hi, :], w1_ref[...], preferred_element_type=F32) + b1_ref[...]
            h = jnp.square(jnp.maximum(h, 0.0))
            part = jnp.dot(h.astype(BF16), w2_ref[...], preferred_element_type=F32)
            pieces = []
            if lo < tm:
                pieces.append((x_ref, o_ref, lo, min(hi, tm), 0))
            if hi > tm:
                pieces.append((xs_ref, os_ref, max(lo, tm) - tm, hi - tm, max(lo, tm) - lo))
            for src, dst, a, b, off in pieces:
                acc = dst[a:b, :] + part[off:off + b - a]
                if finish:
                    v = DN_ALPHA * src[a:b, :] + (acc + b2_ref[...])
                    acc = _layer_norm(v, g_ref[...], b_ref[...])
                dst[a:b, :] = acc

    last = pl.num_programs(1) - 1
    pl.when(f < last)(functools.partial(accumulate, False))
    pl.when(f == last)(functools.partial(accumulate, True))


def _mlp_ln(x2d, xs2d, layer, w1, b1_all, w2, b2_all, g_all, b_all):
    m, bs = x2d.shape[0], xs2d.shape[0]
    tm = min(1024, m)
    tiles = m // tm
    ts = bs // tiles
    chunks = max(tm // MLP_ROWS, 1) + 1
    if bs % tiles or ts % 16 or (tm + ts) % chunks or ((tm + ts) // chunks) % 16:
        return tuple(_mlp_ln_call(a, None, 0, layer, w1, b1_all, w2, b2_all, g_all, b_all)[0] for a in (x2d, xs2d))
    return _mlp_ln_call(x2d, xs2d, ts, layer, w1, b1_all, w2, b2_all, g_all, b_all)


def _mlp_ln_call(x2d, xs2d, ts, layer, w1, b1_all, w2, b2_all, g_all, b_all):
    m = x2d.shape[0]
    tm = min(1024, m)
    tf = 1024
    vec = lambda: pl.BlockSpec((None, 1, D_MODEL), lambda i, f: (layer, 0, 0))
    rows = lambda n: pl.BlockSpec((n, D_MODEL), lambda i, f: (i, 0))
    sample = [xs2d] if ts else []
    return pl.pallas_call(
        functools.partial(_mlp_ln_kernel, sample_rows=ts),
        grid=(m // tm, D_FF // tf),
        in_specs=[rows(tm)] + [rows(ts)] * bool(ts) + [
            pl.BlockSpec((D_MODEL, tf), lambda i, f: (0, f)),
            pl.BlockSpec((None, 1, tf), lambda i, f: (layer, 0, f)),
            pl.BlockSpec((tf, D_MODEL), lambda i, f: (f, 0)),
            vec(), vec(), vec(),
        ],
        out_specs=[rows(tm)] + [rows(ts)] * bool(ts),
        out_shape=[jax.ShapeDtypeStruct(a.shape, F32) for a in [x2d] + sample],
        scratch_shapes=[pltpu.VMEM((tm + ts, D_MODEL), BF16)],
        compiler_params=_params("parallel", "arbitrary"),
        name="mlp_ln",
    )(x2d, *sample, w1, b1_all, w2, b2_all, g_all, b_all)


class _CastJob:
    def __init__(self, sources, steps, step_of):
        nb = 1 << (steps.bit_length() - 1)
        slab = lambda *idx: jnp.minimum(step_of(*idx), nb - 1)
        self.args, self.in_specs, self.out_specs, self.out_shape = [], [], [], []
        for w_all, layer in sources:
            _, rows, cols = w_all.shape
            self.args.append(w_all)
            self.in_specs.append(pl.BlockSpec((None, rows // nb, cols), lambda *idx, l=layer: (l, slab(*idx), 0)))
            self.out_specs.append(pl.BlockSpec((rows // nb, cols), lambda *idx: (slab(*idx), 0)))
            self.out_shape.append(jax.ShapeDtypeStruct((rows, cols), BF16))


def _proj_kernel(x_ref, w_ref, o_ref, xb_ref):
    @pl.when(pl.program_id(1) == 0)
    def _():
        xb_ref[...] = x_ref[...].astype(BF16)

    o_ref[...] = jnp.dot(xb_ref[...], w_ref[...], preferred_element_type=F32)


def _gla_proj(x2d, layer, w_in_all):
    m = x2d.shape[0]
    tm = min(1024, m)
    return pl.pallas_call(
        _proj_kernel,
        grid=(m // tm, PROJ_PAD // PROJ_TILE),
        in_specs=[
            pl.BlockSpec((tm, D_MODEL), lambda i, n: (i, 0)),
            pl.BlockSpec((None, D_MODEL, PROJ_TILE), lambda i, n: (layer, 0, n)),
        ],
        out_specs=pl.BlockSpec((tm, PROJ_TILE), lambda i, n: (i, n)),
        out_shape=jax.ShapeDtypeStruct((m, PROJ_PAD), F32),
        scratch_shapes=[pltpu.VMEM((tm, D_MODEL), BF16)],
        compiler_params=_params("parallel", "arbitrary"),
        name="gla_proj",
    )(x2d, w_in_all)


def _rms_gate(o, og, nw):
    o = o * lax.rsqrt(jnp.mean(o * o, axis=-1, keepdims=True) + RMS_EPS) * nw
    return o * (og * jax.nn.sigmoid(og))


def _gla_block(r0, q_ref, k_ref, v_ref, og_ref, nw_ref, o_ref, s_ref, bc_ref, a_ref):
    scale = GLA_DK ** -0.5
    half = GLA_CHUNK // 2
    chunks = range(CHUNKS_PER_BLOCK)
    blk = slice(r0, r0 + GLA_BLOCK)
    a_blk = a_ref.at[blk]

    tot = [bc_ref[r0 + (c + 1) * GLA_CHUNK - 1:r0 + (c + 1) * GLA_CHUNK, :] for c in chunks]
    before = [jnp.zeros_like(tot[0])]
    for c in chunks[1:]:
        before.append(before[-1] + tot[c - 1])
    after = [jnp.zeros_like(tot[0])]
    for c in reversed(chunks[:-1]):
        after.insert(0, after[0] + tot[c + 1])
    total = before[-1] + tot[-1]

    v = v_ref[blk, :].astype(BF16)
    qs, ks, bcs, qd, kd, q_hi, k_lo = [], [], [], [], [], [], []
    for c in chunks:
        rows = slice(r0 + c * GLA_CHUNK, r0 + (c + 1) * GLA_CHUNK)
        bcc = bc_ref[rows, :]
        qc = q_ref[rows, :] * scale
        kc = k_ref[rows, :]
        mid = bcc[half:half + 1]
        qs.append(qc)
        ks.append(kc)
        bcs.append(bcc)
        qd.append(qc * jnp.exp2(bcc))
        kd.append(kc * jnp.exp2(tot[c] - bcc))
        q_hi.append(qc[half:] * jnp.exp2(bcc[half:] - mid))
        k_lo.append(kc[:half] * jnp.exp2(mid - bcc[:half]))

    s0 = s_ref[...]
    q_in = jnp.concatenate([qd[c] * jnp.exp2(before[c]) for c in chunks], axis=0)
    o = jnp.dot(q_in.astype(BF16), s0.astype(BF16), preferred_element_type=F32)

    kd_all = jnp.concatenate(kd, axis=0).astype(BF16)
    lhs = []
    for jc in chunks[:-1]:
        lhs.append(qd[jc + 1])
        lhs.extend(qd[c] * jnp.exp2(before[c] - before[jc + 1]) for c in chunks[jc + 2:])
    pair = _dot_nt(jnp.concatenate(lhs, axis=0).astype(BF16), kd_all)
    a_blk[...] = jnp.zeros((GLA_BLOCK, GLA_BLOCK), F32)
    off = 0
    for jc in chunks[:-1]:
        lo = (jc + 1) * GLA_CHUNK
        n = GLA_BLOCK - lo
        lane = lax.broadcasted_iota(jnp.int32, (n, GLA_BLOCK), 1)
        keep = (lane >= jc * GLA_CHUNK) & (lane < lo)
        a_blk[lo:, :] += jnp.where(keep, pair[off:off + n], 0.0)
        off += n

    zeros = jnp.zeros((half, GLA_DK), F32)
    k_lo_all = jnp.concatenate([t for c in chunks for t in (k_lo[c], zeros)], axis=0).astype(BF16)
    cross = _dot_nt(jnp.concatenate(q_hi, axis=0).astype(BF16), k_lo_all)
    row = lax.broadcasted_iota(jnp.int32, (half, GLA_DK), 0)
    lane = lax.broadcasted_iota(jnp.int32, (half, GLA_BLOCK), 1)
    for c in chunks:
        for hs in range(2):
            rs = slice(hs * half, (hs + 1) * half)
            qh, kh, bh = qs[c][rs], ks[c][rs], bcs[c][rs]
            col0 = c * GLA_CHUNK + hs * half
            acc = jnp.zeros((half, GLA_BLOCK), F32)
            if hs == 1:
                in_chunk = (lane >= c * GLA_CHUNK) & (lane < col0)
                acc = jnp.where(in_chunk, cross[c * half:(c + 1) * half], 0.0)
            for j in range(half):
                d = bh - bh[j:j + 1]
                e = jnp.exp2(jnp.where(row >= j, d, -jnp.inf)) if j else jnp.exp2(d)
                t = qh * e * kh[j:j + 1]
                acc = acc + jnp.where(lane == col0 + j, jnp.sum(t, axis=1, keepdims=True), 0.0)
            a_blk[col0:col0 + half, :] += acc

    o = o + jnp.dot(a_blk[...].astype(BF16), v, preferred_element_type=F32)
    o_ref[blk, :] = _rms_gate(o, og_ref[blk, :], nw_ref[...]).astype(o_ref.dtype)

    k_out = jnp.concatenate([kd[c] * jnp.exp2(after[c]) for c in chunks], axis=0)
    upd = _dot_tn(k_out.astype(BF16), v)
    gcol = jnp.transpose(jnp.broadcast_to(jnp.exp2(total), (LANES, GLA_DK)))
    for n in range(GLA_DV // LANES):
        cols = slice(n * LANES, (n + 1) * LANES)
        s_ref[:, cols] = s0[:, cols] * gcol + upd[:, cols]


def _gla_rec_kernel(q_ref, k_ref, v_ref, og_ref, gl_ref, wgu_ref, gb_ref, nw_ref, tril_ref, *rest, cast):
    srcs, (o_ref, sout_ref), dsts = rest[:cast], rest[cast:cast + 2], rest[cast + 2:2 * cast + 2]
    s_ref, bc_ref, a_ref = rest[2 * cast + 2:]
    for src_ref, dst_ref in zip(srcs, dsts):
        dst_ref[...] = src_ref[...].astype(BF16)
    l = pl.program_id(2)

    @pl.when(l == 0)
    def _():
        s_ref[...] = jnp.zeros_like(s_ref)

    gk = jnp.dot(gl_ref[...].astype(BF16), wgu_ref[...], preferred_element_type=F32) + gb_ref[...]
    la = _log_sigmoid(gk) * (LOG2_E / GLA_GATE_TEMP)
    blocks = q_ref.shape[0] // GLA_BLOCK
    for i in range(blocks):
        blk = slice(i * GLA_BLOCK, (i + 1) * GLA_BLOCK)
        bc_ref[blk, :] = _mask_dot(tril_ref[...], la[blk], 2)
    for i in range(blocks):
        _gla_block(i * GLA_BLOCK, q_ref, k_ref, v_ref, og_ref, nw_ref, o_ref, s_ref, bc_ref, a_ref)

    @pl.when(l == pl.num_programs(2) - 1)
    def _():
        sout_ref[...] = s_ref[...]


def _chunk_tril():
    i = jnp.arange(GLA_BLOCK)[:, None]
    j = jnp.arange(GLA_BLOCK)[None, :]
    return ((i // GLA_CHUNK == j // GLA_CHUNK) & (j <= i)).astype(BF16)


def _gla_rec_prompt(proj, batch, seq, layer, wgu_all, gb_all, nw_all, cast_sources=()):
    tl = min(1024, seq)
    nl = seq // tl
    kq = GLA_KEY_DIM // GLA_DK
    kv = 2 * GLA_KEY_DIM // GLA_DV
    row = lambda b, h, l: b * nl + l
    job = None
    if cast_sources:
        job = _CastJob(cast_sources, steps=batch * GLA_HEADS * nl,
                       step_of=lambda b, h, l: (b * GLA_HEADS + h) * nl + l)
    outs = pl.pallas_call(
        functools.partial(_gla_rec_kernel, cast=len(cast_sources)),
        grid=(batch, GLA_HEADS, nl),
        in_specs=[
            pl.BlockSpec((tl, GLA_DK), lambda b, h, l: (row(b, h, l), h)),
            pl.BlockSpec((tl, GLA_DK), lambda b, h, l: (row(b, h, l), kq + h)),
            pl.BlockSpec((tl, GLA_DV), lambda b, h, l: (row(b, h, l), kv + h)),
            pl.BlockSpec((tl, GLA_DV), lambda b, h, l: (row(b, h, l), kv + GLA_HEADS + h)),
            pl.BlockSpec((tl, LANES), lambda b, h, l: (row(b, h, l), PROJ_QKVG // LANES)),
            pl.BlockSpec((None, LANES, GLA_DK), lambda b, h, l: (layer, 0, h)),
            pl.BlockSpec((None, 1, GLA_DK), lambda b, h, l: (layer, 0, h)),
            _layer_spec(layer, 1, GLA_DV),
            pl.BlockSpec((GLA_BLOCK, GLA_BLOCK), lambda b, h, l: (0, 0)),
        ] + (job.in_specs if job else []),
        out_specs=[
            pl.BlockSpec((tl, GLA_DV), lambda b, h, l: (row(b, h, l), h)),
            pl.BlockSpec((None, None, GLA_DK, GLA_DV), lambda b, h, l: (b, h, 0, 0)),
        ] + (job.out_specs if job else []),
        out_shape=[
            jax.ShapeDtypeStruct((batch * seq, GLA_VAL_DIM), BF16),
            jax.ShapeDtypeStruct((batch, GLA_HEADS, GLA_DK, GLA_DV), F32),
        ] + (job.out_shape if job else []),
        scratch_shapes=[
            pltpu.VMEM((GLA_DK, GLA_DV), F32),
            pltpu.VMEM((tl, GLA_DK), F32),
            pltpu.VMEM((tl, GLA_BLOCK), F32),
        ],
        compiler_params=_params("arbitrary", "arbitrary", "arbitrary"),
        name="gla_rec_prompt",
    )(proj, proj, proj, proj, proj, wgu_all, gb_all, nw_all, _chunk_tril(), *(job.args if job else []))
    return outs[0], outs[1], tuple(outs[2:])


def _gla_sample_kernel(q_ref, k_ref, v_ref, og_ref, gl_ref, s_ref, wgu_ref, gb_ref, nw_ref, *rest, tb):
    o_ref, sout_ref = rest[-2:]
    gk = jnp.dot(gl_ref[...].astype(BF16), wgu_ref[...], preferred_element_type=F32) + gb_ref[...]
    a = jnp.exp(_log_sigmoid(gk) / GLA_GATE_TEMP)
    q = q_ref[...] * (GLA_DK ** -0.5)
    k = k_ref[...]
    v = v_ref[...]
    qk = jnp.sum(q * k, axis=1, keepdims=True)
    pad = jnp.zeros((LANES - 3 * tb, GLA_DK), F32)
    cols = jnp.transpose(jnp.concatenate([a, k, q * a, pad], axis=0))
    outs = []
    for b in range(tb):
        s0 = s_ref[b]
        vb = v[b:b + 1]
        outs.append(qk[b:b + 1] * vb + jnp.sum(cols[:, 2 * tb + b:2 * tb + b + 1] * s0, axis=0, keepdims=True))
        sout_ref[b] = s0 * cols[:, b:b + 1] + cols[:, tb + b:tb + b + 1] * vb
    o_ref[...] = _rms_gate(jnp.concatenate(outs, axis=0), og_ref[...], nw_ref[...])


def _gla_rec_sample(proj, state_gla, new_state, layer, wgu_all, gb_all, nw_all):
    bs = proj.shape[0]
    tb = min(2 * SUBLANES, bs)
    kq = GLA_KEY_DIM // GLA_DK
    kv = 2 * GLA_KEY_DIM // GLA_DV
    s_spec = pl.BlockSpec((None, tb, None, GLA_DK, GLA_DV), lambda i, h: (layer, i, h, 0, 0))
    in_specs = [
        pl.BlockSpec((tb, GLA_DK), lambda i, h: (i, h)),
        pl.BlockSpec((tb, GLA_DK), lambda i, h: (i, kq + h)),
        pl.BlockSpec((tb, GLA_DV), lambda i, h: (i, kv + h)),
        pl.BlockSpec((tb, GLA_DV), lambda i, h: (i, kv + GLA_HEADS + h)),
        pl.BlockSpec((tb, LANES), lambda i, h: (i, PROJ_QKVG // LANES)),
        s_spec,
        pl.BlockSpec((None, LANES, GLA_DK), lambda i, h: (layer, 0, h)),
        pl.BlockSpec((None, 1, GLA_DK), lambda i, h: (layer, 0, h)),
        _layer_spec(layer, 1, GLA_DV),
    ]
    args = [proj, proj, proj, proj, proj, state_gla, wgu_all, gb_all, nw_all]
    aliases = {}
    if new_state is not None:
        in_specs.append(pl.BlockSpec(memory_space=pl.ANY))
        args.append(new_state)
        aliases = {len(args) - 1: 1}
    return pl.pallas_call(
        functools.partial(_gla_sample_kernel, tb=tb),
        grid=(bs // tb, GLA_HEADS),
        in_specs=in_specs,
        out_specs=[pl.BlockSpec((tb, GLA_DV), lambda i, h: (i, h)), s_spec],
        out_shape=[
            jax.ShapeDtypeStruct((bs, GLA_VAL_DIM), F32),
            jax.ShapeDtypeStruct(state_gla.shape, F32),
        ],
        input_output_aliases=aliases,
        compiler_params=_params("parallel", "parallel"),
        name="gla_rec_sample",
    )(*args)


def _out_ln_kernel(o_ref, w_ref, x_ref, g_ref, b_ref, y_ref):
    tm = x_ref.shape[0]
    rb = min(OUT_ROWS, tm)
    for r in range(tm // rb):
        rows = slice(r * rb, (r + 1) * rb)
        y = jnp.dot(o_ref[rows, :].astype(BF16), w_ref[...], preferred_element_type=F32)
        y_ref[rows, :] = _layer_norm(DN_ALPHA * x_ref[rows, :] + y, g_ref[...], b_ref[...])


def _out_ln(o2d, x2d, mix_layer, w_out, g_all, b_all):
    m = x2d.shape[0]
    tm = min(512, m)
    return pl.pallas_call(
        _out_ln_kernel,
        grid=(m // tm,),
        in_specs=[
            pl.BlockSpec((tm, GLA_VAL_DIM), lambda i: (i, 0)),
            pl.BlockSpec((GLA_VAL_DIM, D_MODEL), lambda i: (0, 0)),
            pl.BlockSpec((tm, D_MODEL), lambda i: (i, 0)),
            _layer_spec(mix_layer, 1, D_MODEL), _layer_spec(mix_layer, 1, D_MODEL),
        ],
        out_specs=pl.BlockSpec((tm, D_MODEL), lambda i: (i, 0)),
        out_shape=jax.ShapeDtypeStruct(x2d.shape, F32),
        compiler_params=_params("parallel"),
        name="gla_out_ln",
    )(o2d, w_out, x2d, g_all, b_all)


def kernel(x_prompt, x_sample, state_pool, state_gla, pool_w, pool_scale, gla_w_in, gla_w_gate_up,
           gla_gate_bias, gla_norm_w, gla_w_out, ln_mix_g, ln_mix_b, mlp_w1, mlp_b1, mlp_w2, mlp_b2,
           ln_ffn_g, ln_ffn_b):
    batch, seq, _ = x_prompt.shape
    bs = x_sample.shape[0]
    xp = x_prompt.reshape(batch * seq, D_MODEL)
    xs = x_sample.reshape(bs, D_MODEL)

    vec = lambda a: a.reshape(a.shape[0], 1, a.shape[1])
    mlp_w = {}
    w_in = jnp.pad(gla_w_in.astype(BF16), ((0, 0), (0, 0), (0, PROJ_PAD - gla_w_in.shape[-1])))
    wgu = jnp.pad(gla_w_gate_up.astype(BF16), ((0, 0), (0, LANES - GLA_GATE_RANK), (0, 0)))
    pool_scale, gate_bias, norm_w, mix_g, mix_b, b1, b2, ffn_g, ffn_b = map(
        vec, (pool_scale, gla_gate_bias, gla_norm_w, ln_mix_g, ln_mix_b, mlp_b1, mlp_b2, ln_ffn_g, ln_ffn_b))

    state_rows = jnp.transpose(state_pool, (0, 2, 1, 3))
    pool_in_p, gla_states_p = [], []
    new_rows = None
    new_gla_sample = None
    for i in range(DEPTH):
        j = i // 2
        if i % 2 == 0:
            pool_in_p.append(xp.reshape(batch, seq, D_MODEL)[:, seq - POOL_CTX:])
            sources = [] if i in mlp_w else [(mlp_w1, i), (mlp_w2, i)]
            xp, cast = _pool_ln_prompt(xp, batch, seq, j, i, pool_w, pool_scale, mix_g, mix_b, sources)
            if sources:
                mlp_w[i] = cast
            xs, new_rows = _pool_ln_sample(xs, state_rows, new_rows, j, i, pool_w, pool_scale, mix_g, mix_b)
        else:
            mlp_layers = [n for n in (i, i + 1) if n < DEPTH]
            sources = [(gla_w_out, j)] + [(w, n) for n in mlp_layers for w in (mlp_w1, mlp_w2)]
            o_p, s_p, cast = _gla_rec_prompt(_gla_proj(xp, j, w_in), batch, seq, j, wgu, gate_bias, norm_w, sources)
            w_out = cast[0]
            for n, pair in zip(mlp_layers, zip(cast[1::2], cast[2::2])):
                mlp_w[n] = pair
            gla_states_p.append(s_p)
            o_s, new_gla_sample = _gla_rec_sample(_gla_proj(xs, j, w_in), state_gla, new_gla_sample, j, wgu,
                                                  gate_bias, norm_w)
            xp = _out_ln(o_p, xp, i, w_out, mix_g, mix_b)
            xs = _out_ln(o_s, xs, i, w_out, mix_g, mix_b)
        w1, w2 = mlp_w[i]
        xp, xs = _mlp_ln(xp, xs, i, w1, b1, w2, b2, ffn_g, ffn_b)

    new_pool_sample = jnp.transpose(new_rows, (0, 2, 1, 3))
    return (xp.reshape(batch, seq, D_MODEL), xs.reshape(bs, 1, D_MODEL), jnp.stack(pool_in_p),
            jnp.stack(gla_states_p), new_pool_sample, new_gla_sample)
```
